```python
import jax, jax.numpy as jnp
from jax import lax
import numpy as np

D_MODEL = 2048
BATCH = 8
SEQ = 4096
DEPTH = 1

PLE_DIM = 256
ATTN_WIDTH = D_MODEL // 2
CONV_WIDTH = D_MODEL - ATTN_WIDTH
HEAD_DIM = 64
N_Q_HEADS = ATTN_WIDTH // HEAD_DIM
N_KV_HEADS = 4
GQA_GROUP = N_Q_HEADS // N_KV_HEADS
KV_WIDTH = N_KV_HEADS * HEAD_DIM
WINDOW = 128
BLOCK = 128
ROT_DIM = HEAD_DIM // 4
ROPE_THETA = 500000.0
CONV_K = 3
CONV_GROUPS = 16
EPS = 1e-6
NEG_INF = -1e30

SEG_WIDTHS = (ATTN_WIDTH, KV_WIDTH, KV_WIDTH, ATTN_WIDTH,
              CONV_WIDTH, CONV_WIDTH, CONV_WIDTH, CONV_WIDTH)
IN_WIDTH = sum(SEG_WIDTHS)
SPLITS = [int(v) for v in np.cumsum(SEG_WIDTHS)[:-1]]

kernel_name = "hymba_swa_sink_shortconv_ple"


def rms_norm(x, gain):
    xf = x.astype(jnp.float32)
    y = xf * lax.rsqrt(jnp.mean(xf * xf, axis=-1, keepdims=True) + EPS)
    return (y * gain.astype(jnp.float32)).astype(x.dtype)


def partial_rope(x, pos):
    half = ROT_DIM // 2
    inv_freq = jnp.power(jnp.float32(ROPE_THETA), -jnp.arange(half, dtype=jnp.float32) * 2.0 / ROT_DIM)
    ang = pos.astype(jnp.float32)[:, None] * inv_freq[None, :]
    cos = jnp.cos(ang)[None, :, None, :]
    sin = jnp.sin(ang)[None, :, None, :]
    xr = x[..., :ROT_DIM].astype(jnp.float32)
    x1, x2 = xr[..., :half], xr[..., half:]
    rot = jnp.concatenate([x1 * cos - x2 * sin, x2 * cos + x1 * sin], axis=-1).astype(x.dtype)
    return jnp.concatenate([rot, x[..., ROT_DIM:]], axis=-1)


def sliding_window_attention(q, k, v, sinks):
    b, s = q.shape[0], q.shape[1]
    nb = s // BLOCK
    qb = q.reshape(b, nb, BLOCK, N_KV_HEADS, GQA_GROUP, HEAD_DIM)

    def band(t):
        tb = t.reshape(b, nb, BLOCK, N_KV_HEADS, HEAD_DIM)
        prev = jnp.pad(tb, ((0, 0), (1, 0), (0, 0), (0, 0), (0, 0)))[:, :-1]
        return jnp.concatenate([prev, tb], axis=2)

    kb, vb = band(k), band(v)
    scores = jnp.einsum('bnqhgd,bnkhd->bnhgqk', qb, kb,
                        preferred_element_type=jnp.float32) * (HEAD_DIM ** -0.5)
    q_pos = jnp.arange(nb)[:, None] * BLOCK + jnp.arange(BLOCK)[None, :]
    k_pos = jnp.arange(nb)[:, None] * BLOCK - BLOCK + jnp.arange(2 * BLOCK)[None, :]
    diff = q_pos[:, :, None] - k_pos[:, None, :]
    mask = (diff >= 0) & (diff < WINDOW) & (k_pos[:, None, :] >= 0)
    scores = jnp.where(mask[None, :, None, None], scores, NEG_INF)
    sink = sinks.astype(jnp.float32).reshape(N_KV_HEADS, GQA_GROUP)[None, None, :, :, None, None]
    m = jnp.maximum(jnp.max(scores, axis=-1, keepdims=True), sink)
    e = jnp.exp(scores - m)
    probs = e / (jnp.sum(e, axis=-1, keepdims=True) + jnp.exp(sink - m))
    out = jnp.einsum('bnhgqk,bnkhd->bnqhgd', probs.astype(v.dtype), vb)
    return out.reshape(b, s, N_Q_HEADS * HEAD_DIM)


def short_conv(u, w):
    return lax.conv_general_dilated(
        u, w[:, None, :].astype(u.dtype), window_strides=(1,),
        padding=[(CONV_K - 1, 0)], dimension_numbers=('NWC', 'WIO', 'NWC'),
        feature_group_count=u.shape[-1])


def setup_inputs(seed: int = 0) -> dict:
    key = jax.random.key(seed)
    ks = jax.random.split(key, 16)
    f32 = jnp.float32
    nrm = lambda k, shape, scale: jax.random.normal(k, shape, f32) * scale
    return {
        "x": nrm(ks[0], (BATCH, SEQ, D_MODEL), 1.0),
        "p": nrm(ks[1], (DEPTH, BATCH, SEQ, PLE_DIM), 1.0),
        "norm_gain": 1.0 + nrm(ks[2], (DEPTH, D_MODEL), 0.02),
        "w_in": nrm(ks[3], (DEPTH, D_MODEL, IN_WIDTH), D_MODEL ** -0.5),
        "q_norm_gain": 1.0 + nrm(ks[4], (DEPTH, HEAD_DIM), 0.02),
        "k_norm_gain": 1.0 + nrm(ks[5], (DEPTH, HEAD_DIM), 0.02),
        "attn_sinks": nrm(ks[6], (DEPTH, N_Q_HEADS), 0.5),
        "conv_w": nrm(ks[7], (DEPTH, CONV_K, CONV_WIDTH), CONV_K ** -0.5),
        "w_out": nrm(ks[8], (DEPTH, D_MODEL, D_MODEL), D_MODEL ** -0.5),
        "ple_gate_norm_gain": 1.0 + nrm(ks[9], (DEPTH, D_MODEL), 0.02),
        "w_ple_gate": nrm(ks[10], (DEPTH, D_MODEL, D_MODEL), D_MODEL ** -0.5),
        "b_ple_gate": nrm(ks[11], (DEPTH, D_MODEL), 0.01),
        "w_ple_proj": nrm(ks[12], (DEPTH, PLE_DIM, D_MODEL), PLE_DIM ** -0.5),
        "ple_norm_gain": 1.0 + nrm(ks[13], (DEPTH, D_MODEL), 0.02),
    }


def reference(x, p, norm_gain, w_in, q_norm_gain, k_norm_gain, attn_sinks, conv_w, w_out,
              ple_gate_norm_gain, w_ple_gate, b_ple_gate, w_ple_proj, ple_norm_gain):
    b, s = x.shape[0], x.shape[1]
    pos = jnp.arange(s)
    for i in range(DEPTH):
        h = rms_norm(x, norm_gain[i])
        z = h @ w_in[i]
        q, k, v, g_attn, c_b, c_c, c_h, g_conv = jnp.split(z, SPLITS, axis=-1)

        q = q.reshape(b, s, N_Q_HEADS, HEAD_DIM)
        k = k.reshape(b, s, N_KV_HEADS, HEAD_DIM)
        v = v.reshape(b, s, N_KV_HEADS, HEAD_DIM)
        q = partial_rope(rms_norm(q, q_norm_gain[i]), pos)
        k = partial_rope(rms_norm(k, k_norm_gain[i]), pos)
        y_attn = sliding_window_attention(q, k, v, attn_sinks[i]) * jax.nn.silu(g_attn)

        y_conv = c_b * short_conv(c_c * c_h, conv_w[i]) * jax.nn.silu(g_conv)

        mix = jnp.concatenate([y_attn, y_conv], axis=-1)
        x = x + mix @ w_out[i]

        gate = jax.nn.sigmoid(rms_norm(x, ple_gate_norm_gain[i]) @ w_ple_gate[i] + b_ple_gate[i])
        e = rms_norm(p[i] @ w_ple_proj[i], ple_norm_gain[i])
        x = x + gate * e
    return x
```

```python
import functools

import jax
import jax.numpy as jnp
import numpy as np
from jax import lax
from jax.experimental import pallas as pl
from jax.experimental.pallas import tpu as pltpu

D_MODEL = 2048
PLE_DIM = 256
ATTN_WIDTH = 1024
CONV_WIDTH = 1024
HEAD_DIM = 64
N_Q_HEADS = 16
N_KV_HEADS = 4
KV_WIDTH = 256
WINDOW = 128
ROT_DIM = 16
ROPE_THETA = 500000.0
CONV_K = 3
EPS = 1e-6
NEG_INF = -1e30
IN_WIDTH = 2 * ATTN_WIDTH + 2 * KV_WIDTH + 4 * CONV_WIDTH

LANES = 128
PAIRS = ATTN_WIDTH // LANES
QBLK = 128

COL_Q, COL_GA, COL_B, COL_C, COL_H, COL_GC = 0, 1, 2, 3, 4, 5
COL_KV_512 = 12

F32 = jnp.float32
BF16 = jnp.bfloat16


def _pair_perm_cols(w):
    lead = w.shape[:-1]
    w = w.reshape(*lead, 2, 2, 4, HEAD_DIM)
    w = jnp.swapaxes(w, -3, -2)
    return w.reshape(*lead, ATTN_WIDTH)


def _inproj_kernel(x_ref, g_ref, w_ref, z_ref, h_ref):
    @pl.when(pl.program_id(1) == 0)
    def _():
        x = x_ref[...]
        ms = jnp.mean(x * x, axis=-1, keepdims=True)
        h_ref[...] = (x * lax.rsqrt(ms + EPS) * g_ref[...]).astype(BF16)

    z_ref[...] = jnp.dot(h_ref[...], w_ref[...], preferred_element_type=F32)


def _inproj(x2, gain, w_in_p, tm, tn):
    t = x2.shape[0]
    return pl.pallas_call(
        _inproj_kernel,
        grid=(t // tm, IN_WIDTH // tn),
        in_specs=[
            pl.BlockSpec((tm, D_MODEL), lambda i, j: (i, 0)),
            pl.BlockSpec((1, D_MODEL), lambda i, j: (0, 0)),
            pl.BlockSpec((D_MODEL, tn), lambda i, j: (0, j)),
        ],
        out_specs=pl.BlockSpec((tm, tn), lambda i, j: (i, j)),
        out_shape=jax.ShapeDtypeStruct((t, IN_WIDTH), F32),
        scratch_shapes=[pltpu.VMEM((tm, D_MODEL), BF16)],
        compiler_params=pltpu.CompilerParams(
            dimension_semantics=("arbitrary", "arbitrary"),
            vmem_limit_bytes=48 * 1024 * 1024),
        name="inproj",
    )(x2, gain, w_in_p)


def _mixer_kernel(sinks_ref, zq_ref, zg_ref, zb_ref, zc_ref, zh_ref, zgc_ref, zkv_ref,
                  rc_ref, rs1_ref, rs2_ref, qg_ref, kg_ref, cw_ref, mix_ref,
                  kband, vband, u_scr):
    s = pl.program_id(1)

    @pl.when(s == 0)
    def _():
        kband[:, 0:QBLK, :] = jnp.zeros((2, QBLK, LANES), BF16)
        vband[0:QBLK, :] = jnp.zeros((QBLK, KV_WIDTH), BF16)
        u_scr[0:8, :] = jnp.zeros((8, CONV_WIDTH), F32)

    lane = lax.broadcasted_iota(jnp.int32, (1, LANES), 1)
    lo = lane < HEAD_DIM
    rr = lax.broadcasted_iota(jnp.int32, (LANES, LANES), 0) // HEAD_DIM
    cc = lax.broadcasted_iota(jnp.int32, (LANES, LANES), 1) // HEAD_DIM
    bd = jnp.where(rr == cc, 1.0, 0.0).astype(BF16)
    rc, rs1, rs2 = rc_ref[...], rs1_ref[...], rs2_ref[...]

    def head_norm_rope(t, gain):
        sq = t * t
        hi = sq.astype(BF16)
        rest = (sq - hi.astype(F32)).astype(BF16)
        ss = (jnp.dot(hi, bd, preferred_element_type=F32)
              + jnp.dot(rest, bd, preferred_element_type=F32))
        y = t * lax.rsqrt(ss * (1.0 / HEAD_DIM) + EPS) * gain
        return y * rc + pltpu.roll(y, LANES - ROT_DIM // 2, 1) * rs1 + pltpu.roll(y, ROT_DIM // 2, 1) * rs2

    kg = kg_ref[...]
    qg = qg_ref[...]
    for a in range(2):
        kband[a, QBLK:2 * QBLK, :] = head_norm_rope(zkv_ref[:, a * LANES:(a + 1) * LANES], kg).astype(BF16)
    vband[QBLK:2 * QBLK, :] = zkv_ref[:, KV_WIDTH:2 * KV_WIDTH].astype(BF16)

    qi = lax.broadcasted_iota(jnp.int32, (QBLK, 2 * QBLK), 0)
    ci = lax.broadcasted_iota(jnp.int32, (QBLK, 2 * QBLK), 1)
    mask = (ci > qi) & (ci <= qi + QBLK) & ((ci >= QBLK) | (s > 0))

    nt = (((1,), (1,)), ((), ()))
    for a in range(2):
        kt = kband[a]
        zero = jnp.zeros_like(kt)
        k_lo = jnp.where(lo, kt, zero)
        k_hi = jnp.where(lo, zero, kt)
        vt = vband[:, a * LANES:(a + 1) * LANES]
        v2 = jnp.concatenate([jnp.where(lo, vt, zero), jnp.where(lo, zero, vt)], axis=0)
        q = jnp.concatenate(
            [head_norm_rope(zq_ref[:, p * LANES:(p + 1) * LANES], qg).astype(BF16)
             for p in range(4 * a, 4 * a + 4)], axis=0)
        s_lo = lax.dot_general(q, k_lo, nt, preferred_element_type=F32)
        s_hi = lax.dot_general(q, k_hi, nt, preferred_element_type=F32)
        for c in range(4):
            p = 4 * a + c
            es, inv = [], []
            for half, sc_all in ((0, s_lo), (1, s_hi)):
                sink = sinks_ref[8 * a + 4 * half + c]
                sc = jnp.where(mask, sc_all[c * QBLK:(c + 1) * QBLK], NEG_INF)
                m = jnp.maximum(jnp.max(sc, axis=-1, keepdims=True), sink)
                e = jnp.exp(sc - m)
                den = jnp.sum(e, axis=-1, keepdims=True) + jnp.exp(sink - m)
                es.append(e.astype(BF16))
                inv.append(1.0 / den)
            o = jnp.dot(jnp.concatenate(es, axis=1), v2, preferred_element_type=F32)
            o = o * jnp.where(lo, inv[0], inv[1])
            g = zg_ref[:, p * LANES:(p + 1) * LANES]
            mix_ref[:, p * LANES:(p + 1) * LANES] = (o * (g * jax.nn.sigmoid(g))).astype(BF16)

    u = zc_ref[...] * zh_ref[...]
    u_scr[8:8 + QBLK, :] = u
    cw = cw_ref[...]
    conv = cw[0:1] * u_scr[6:6 + QBLK, :] + cw[1:2] * u_scr[7:7 + QBLK, :] + cw[2:3] * u
    gc = zgc_ref[...]
    mix_ref[:, ATTN_WIDTH:] = (zb_ref[...] * conv * (gc * jax.nn.sigmoid(gc))).astype(BF16)

    kband[:, 0:QBLK, :] = kband[:, QBLK:2 * QBLK, :]
    vband[0:QBLK, :] = vband[QBLK:2 * QBLK, :]
    u_scr[0:8, :] = u_scr[QBLK:QBLK + 8, :]


def _mixers(z3, sinks, tabs, qg, kg, conv_w):
    b, s, _ = z3.shape
    zblk = lambda col: pl.BlockSpec((None, QBLK, 1024), lambda bi, si, col=col: (bi, si, col))
    tab = pl.BlockSpec((QBLK, LANES), lambda bi, si: (si, 0))
    vec = lambda r, w: pl.BlockSpec((r, w), lambda bi, si: (0, 0))
    return pl.pallas_call(
        _mixer_kernel,
        grid=(b, s // QBLK),
        in_specs=[
            pl.BlockSpec(memory_space=pltpu.SMEM),
            zblk(COL_Q), zblk(COL_GA), zblk(COL_B), zblk(COL_C), zblk(COL_H), zblk(COL_GC),
            pl.BlockSpec((None, QBLK, 2 * KV_WIDTH), lambda bi, si: (bi, si, COL_KV_512)),
            tab, tab, tab,
            vec(1, LANES), vec(1, LANES), vec(CONV_K, CONV_WIDTH),
        ],
        out_specs=pl.BlockSpec((None, QBLK, D_MODEL), lambda bi, si: (bi, si, 0)),
        out_shape=jax.ShapeDtypeStruct((b, s, D_MODEL), BF16),
        scratch_shapes=[
            pltpu.VMEM((2, 2 * QBLK, LANES), BF16),
            pltpu.VMEM((2 * QBLK, KV_WIDTH), BF16),
            pltpu.VMEM((QBLK + 8, CONV_WIDTH), F32),
        ],
        compiler_params=pltpu.CompilerParams(
            dimension_semantics=("arbitrary", "arbitrary"),
            vmem_limit_bytes=32 * 1024 * 1024),
        name="mixers",
    )(sinks, z3, z3, z3, z3, z3, z3, z3, *tabs, qg, kg, conv_w)


def _out_kernel(mix_ref, x_ref, p_ref, wo_ref, g2_ref, wg_ref, bg_ref, wp_ref, g3_ref, o_ref):
    x1 = x_ref[...] + jnp.dot(mix_ref[...], wo_ref[...], preferred_element_type=F32)
    ms = jnp.mean(x1 * x1, axis=-1, keepdims=True)
    hn = (x1 * lax.rsqrt(ms + EPS) * g2_ref[...]).astype(BF16)
    gate = jax.nn.sigmoid(jnp.dot(hn, wg_ref[...], preferred_element_type=F32) + bg_ref[...])
    pe = jnp.dot(p_ref[...].astype(BF16), wp_ref[...], preferred_element_type=F32)
    e = pe * lax.rsqrt(jnp.mean(pe * pe, axis=-1, keepdims=True) + EPS) * g3_ref[...]
    o_ref[...] = x1 + gate * e


def _out_stage(mix2, x2, p2, w_out_p, g2, w_gate, b_gate, w_proj, g3, tm):
    t = x2.shape[0]
    row = lambda w: pl.BlockSpec((tm, w), lambda i: (i, 0))
    const = lambda r, c: pl.BlockSpec((r, c), lambda i: (0, 0), pipeline_mode=pl.Buffered(1))
    return pl.pallas_call(
        _out_kernel,
        grid=(t // tm,),
        in_specs=[
            row(D_MODEL), row(D_MODEL), row(PLE_DIM),
            const(D_MODEL, D_MODEL), const(1, D_MODEL),
            const(D_MODEL, D_MODEL), const(1, D_MODEL),
            const(PLE_DIM, D_MODEL), const(1, D_MODEL),
        ],
        out_specs=row(D_MODEL),
        out_shape=jax.ShapeDtypeStruct((t, D_MODEL), F32),
        compiler_params=pltpu.CompilerParams(
            dimension_semantics=("arbitrary",),
            vmem_limit_bytes=56 * 1024 * 1024),
        name="outproj_ple",
    )(mix2, x2, p2, w_out_p, g2, w_gate, b_gate, w_proj, g3)


def _rope_tables(s):
    half = ROT_DIM // 2
    inv_freq = jnp.power(jnp.float32(ROPE_THETA), -jnp.arange(half, dtype=F32) * 2.0 / ROT_DIM)
    ang = jnp.arange(s).astype(F32)[:, None] * inv_freq[None, :]
    cos, sin = jnp.cos(ang), jnp.sin(ang)
    pad = lambda n: jnp.zeros((s, n), F32)
    c64 = jnp.concatenate([cos, cos, jnp.ones((s, HEAD_DIM - ROT_DIM), F32)], axis=1)
    s1 = jnp.concatenate([-sin, pad(HEAD_DIM - half)], axis=1)
    s2 = jnp.concatenate([pad(half), sin, pad(HEAD_DIM - ROT_DIM)], axis=1)
    return tuple(jnp.tile(t, (1, LANES // HEAD_DIM)) for t in (c64, s1, s2))


def _layer(x, p_i, norm_gain, w_in, q_gain, k_gain, sinks, conv_w, w_out,
           gate_norm_gain, w_gate, b_gate, w_proj, ple_norm_gain):
    b, s, _ = x.shape
    t = b * s
    sq, sk, sv, sg, sb, sc, sh = np.cumsum(
        [ATTN_WIDTH, KV_WIDTH, KV_WIDTH, ATTN_WIDTH, CONV_WIDTH, CONV_WIDTH, CONV_WIDTH])
    w_in_p = jnp.concatenate([
        _pair_perm_cols(w_in[:, :sq]), _pair_perm_cols(w_in[:, sv:sg]),
        w_in[:, sg:sb], w_in[:, sb:sc], w_in[:, sc:sh], w_in[:, sh:],
        w_in[:, sq:sk], w_in[:, sk:sv]], axis=1).astype(BF16)
    w_out_p = jnp.concatenate([
        _pair_perm_cols(w_out[:ATTN_WIDTH].T).T, w_out[ATTN_WIDTH:]], axis=0).astype(BF16)

    x2 = x.reshape(t, D_MODEL)
    tm1 = min(1024, t)
    z = _inproj(x2, norm_gain.reshape(1, D_MODEL), w_in_p, tm1, 512)

    qg = jnp.tile(q_gain * (HEAD_DIM ** -0.5), LANES // HEAD_DIM).reshape(1, LANES)
    kg = jnp.tile(k_gain, LANES // HEAD_DIM).reshape(1, LANES)
    mix = _mixers(z.reshape(b, s, IN_WIDTH), sinks, _rope_tables(s), qg, kg, conv_w)

    tm3 = min(512, t)
    out = _out_stage(mix.reshape(t, D_MODEL), x2, p_i.reshape(t, PLE_DIM), w_out_p,
                     gate_norm_gain.reshape(1, D_MODEL), w_gate.astype(BF16),
                     b_gate.reshape(1, D_MODEL), w_proj.astype(BF16),
                     ple_norm_gain.reshape(1, D_MODEL), tm3)
    return out.reshape(b, s, D_MODEL)


def kernel(x, p, norm_gain, w_in, q_norm_gain, k_norm_gain, attn_sinks, conv_w, w_out,
           ple_gate_norm_gain, w_ple_gate, b_ple_gate, w_ple_proj, ple_norm_gain):
    for i in range(p.shape[0]):
        x = _layer(x, p[i], norm_gain[i], w_in[i], q_norm_gain[i], k_norm_gain[i], attn_sinks[i],
                   conv_w[i], w_out[i], ple_gate_norm_gain[i], w_ple_gate[i], b_ple_gate[i],
                   w_ple_proj[i], ple_norm_gain[i])
    return x
```

```python
import functools
import itertools

import jax
import jax.numpy as jnp
import numpy as np
from jax import lax
from jax.experimental import pallas as pl
from jax.experimental.pallas import tpu as pltpu

D_MODEL = 2048
PLE_DIM = 256
ATTN_WIDTH = 1024
CONV_WIDTH = 1024
HEAD_DIM = 64
N_Q_HEADS = 16
N_KV_HEADS = 4
KV_WIDTH = 256
WINDOW = 128
ROT_DIM = 16
ROPE_THETA = 500000.0
CONV_K = 3
EPS = 1e-6
NEG_INF = -1e30
IN_WIDTH = 2 * ATTN_WIDTH + 2 * KV_WIDTH + 4 * CONV_WIDTH

LANES = 128
PAIRS = ATTN_WIDTH // LANES
QBLK = 128

COL_Q, COL_GA, COL_B, COL_C, COL_H, COL_GC, COL_K, COL_V = 0, 1024, 2048, 3072, 4096, 5120, 6144, 6400
PROJ_CHUNK = 512
FRONT_TM = 256
FILL_PIECES = 4

F32 = jnp.float32
BF16 = jnp.bfloat16


def _pair_perm_cols(w):
    lead = w.shape[:-1]
    w = w.reshape(*lead, 2, 2, 4, HEAD_DIM)
    w = jnp.swapaxes(w, -3, -2)
    return w.reshape(*lead, ATTN_WIDTH)


def _mixer_steps(sinks_ref, z_ref, tabs, qg, kg, mix_ref, kband, vband, first_of_seq, tm):
    lane = lax.broadcasted_iota(jnp.int32, (1, LANES), 1)
    lo = lane < HEAD_DIM
    rr = (lax.broadcasted_iota(jnp.int32, (2 * LANES, LANES), 0) % LANES) // HEAD_DIM
    cc = lax.broadcasted_iota(jnp.int32, (2 * LANES, LANES), 1) // HEAD_DIM
    bd = jnp.where(rr == cc, 1.0, 0.0).astype(BF16)

    def head_sumsq(t):
        sq = t * t
        hi = sq.astype(BF16)
        rest = (sq - hi.astype(F32)).astype(BF16)
        return jnp.dot(jnp.concatenate([hi, rest], axis=1), bd, preferred_element_type=F32)

    def head_norm_rope(t, ss, gain, tab):
        rc, rs1, rs2 = tab
        y = t * lax.rsqrt(ss * (1.0 / HEAD_DIM) + EPS) * gain
        return y * rc + pltpu.roll(y, LANES - ROT_DIM // 2, 1) * rs1 + pltpu.roll(y, ROT_DIM // 2, 1) * rs2

    zcol = lambda col, r0, r1: z_ref[r0:r1, col:col + LANES]
    qi = lax.broadcasted_iota(jnp.int32, (QBLK, 2 * QBLK), 0)
    ci = lax.broadcasted_iota(jnp.int32, (QBLK, 2 * QBLK), 1)
    band = (ci > qi) & (ci <= qi + QBLK)
    nt = (((1,), (1,)), ((), ()))

    for qb in range(tm // QBLK):
        r0, r1 = qb * QBLK, (qb + 1) * QBLK
        tab = tuple(t[r0:r1] for t in tabs)
        mask = band & ((ci >= QBLK) | jnp.logical_not(first_of_seq)) if qb == 0 else band
        for a in range(2):
            if a == 0:
                k_raw = [zcol(COL_K + j * LANES, r0, r1) for j in range(2)]
                k_ss = [head_sumsq(t) for t in k_raw]
            q_raw = [zcol(COL_Q + p * LANES, r0, r1) for p in range(4 * a, 4 * a + 4)]
            q_ss = [head_sumsq(t) for t in q_raw]
            yield 1

            if a == 0:
                for j in range(2):
                    kband[j, QBLK + r0:QBLK + r1, :] = head_norm_rope(k_raw[j], k_ss[j], kg, tab).astype(BF16)
                vband[QBLK + r0:QBLK + r1, :] = z_ref[r0:r1, COL_V:COL_V + KV_WIDTH].astype(BF16)
            kt = kband[a, r0:r0 + 2 * QBLK, :]
            zero = jnp.zeros_like(kt)
            k_lo = jnp.where(lo, kt, zero)
            k_hi = jnp.where(lo, zero, kt)
            vt = vband[r0:r0 + 2 * QBLK, a * LANES:(a + 1) * LANES]
            v2 = jnp.concatenate([jnp.where(lo, vt, zero), jnp.where(lo, zero, vt)], axis=0)
            q = jnp.concatenate(
                [head_norm_rope(t, ss, qg, tab).astype(BF16) for t, ss in zip(q_raw, q_ss)], axis=0)
            s_lo = lax.dot_general(q, k_lo, nt, preferred_element_type=F32)
            s_hi = lax.dot_general(q, k_hi, nt, preferred_element_type=F32)
            yield 2

            for c in range(4):
                p = 4 * a + c
                es, inv = [], []
                for half, sc_all in ((0, s_lo), (1, s_hi)):
                    sink = sinks_ref[8 * a + 4 * half + c]
                    sc = jnp.where(mask, sc_all[c * QBLK:(c + 1) * QBLK], NEG_INF)
                    m = jnp.maximum(jnp.max(sc, axis=-1, keepdims=True), sink)
                    e = jnp.exp(sc - m)
                    den = jnp.sum(e, axis=-1, keepdims=True) + jnp.exp(sink - m)
                    es.append(e.astype(BF16))
                    inv.append(1.0 / den)
                o = jnp.dot(jnp.concatenate(es, axis=1), v2, preferred_element_type=F32)
                o = o * jnp.where(lo, inv[0], inv[1])
                g = zcol(COL_GA + p * LANES, r0, r1)
                mix_ref[r0:r1, p * LANES:(p + 1) * LANES] = (o * (g * jax.nn.sigmoid(g))).astype(BF16)

    kband[:, 0:QBLK, :] = kband[:, tm:tm + QBLK, :]
    vband[0:QBLK, :] = vband[tm:tm + QBLK, :]


def _conv_piece(z_ref, cw, mix_ref, u_scr, tm, c0, c1):
    seg = lambda col: z_ref[:, col + c0:col + c1]
    u = seg(COL_C) * seg(COL_H)
    u_scr[8:8 + tm, c0:c1] = u
    conv = (cw[0:1, c0:c1] * u_scr[6:6 + tm, c0:c1] + cw[1:2, c0:c1] * u_scr[7:7 + tm, c0:c1]
            + cw[2:3, c0:c1] * u)
    gc = seg(COL_GC)
    mix_ref[:, ATTN_WIDTH + c0:ATTN_WIDTH + c1] = (seg(COL_B) * conv * (gc * jax.nn.sigmoid(gc))).astype(BF16)
    u_scr[0:8, c0:c1] = u_scr[tm:tm + 8, c0:c1]


def _front_kernel(sinks_ref, x_ref, g_ref, w_ref, rc_ref, rs1_ref, rs2_ref, qg_ref, kg_ref, cw_ref,
                  mix_ref, h0, h1, z0, z1, kband, vband, u_scr, *, tm, blocks_per_seq):
    i = pl.program_id(0)
    first_of_seq = (i + 2 * blocks_per_seq - 2) % blocks_per_seq == 0

    @pl.when(i == 0)
    def _():
        h1[...] = jnp.zeros_like(h1)
        z0[...] = jnp.zeros_like(z0)
        kband[...] = jnp.zeros_like(kband)
        vband[...] = jnp.zeros_like(vband)
        u_scr[...] = jnp.zeros_like(u_scr)

    @pl.when(first_of_seq)
    def _():
        kband[:, 0:QBLK, :] = jnp.zeros((2, QBLK, LANES), BF16)
        vband[0:QBLK, :] = jnp.zeros((QBLK, KV_WIDTH), BF16)
        u_scr[0:8, :] = jnp.zeros((8, CONV_WIDTH), F32)

    def body(h_w, h_r, z_w, z_r):
        def norm_piece(r0, r1):
            x = x_ref[r0:r1, :]
            ms = jnp.mean(x * x, axis=-1, keepdims=True)
            h_w[r0:r1, :] = (x * lax.rsqrt(ms + EPS) * g_ref[...]).astype(BF16)

        fillers = []
        for j in range(FILL_PIECES):
            cw_ = CONV_WIDTH // FILL_PIECES
            rw_ = tm // FILL_PIECES
            fillers.append(functools.partial(_conv_piece, z_r, cw_ref[...], mix_ref, u_scr, tm, j * cw_, (j + 1) * cw_))
            fillers.append(functools.partial(norm_piece, j * rw_, (j + 1) * rw_))
        fillers = iter(fillers)
        chunks = iter(range(IN_WIDTH // PROJ_CHUNK))

        def project(count):
            for n in itertools.islice(chunks, count):
                cols = slice(n * PROJ_CHUNK, (n + 1) * PROJ_CHUNK)
                z_w[:, cols] = jnp.dot(h_r[...], w_ref[:, cols], preferred_element_type=F32)

        for count in _mixer_steps(sinks_ref, z_r, (rc_ref[...], rs1_ref[...], rs2_ref[...]), qg_ref[...],
                                  kg_ref[...], mix_ref, kband, vband, first_of_seq, tm):
            project(count)
            for fill in itertools.islice(fillers, 1):
                fill()
        project(None)
        for fill in fillers:
            fill()

    @pl.when(i % 2 == 0)
    def _():
        body(h0, h1, z1, z0)

    @pl.when(i % 2 == 1)
    def _():
        body(h1, h0, z0, z1)


def _front(x2, gain, w_in_p, sinks, tabs, qg, kg, conv_w, seq, tm):
    t = x2.shape[0]
    nblk = t // tm
    bps = seq // tm
    const = lambda r, c: pl.BlockSpec((r, c), lambda i: (0, 0), pipeline_mode=pl.Buffered(1))
    tab = pl.BlockSpec((tm, LANES), lambda i: ((i + 2 * bps - 2) % bps, 0))
    return pl.pallas_call(
        functools.partial(_front_kernel, tm=tm, blocks_per_seq=bps),
        grid=(nblk + 2,),
        in_specs=[
            pl.BlockSpec(memory_space=pltpu.SMEM),
            pl.BlockSpec((tm, D_MODEL), lambda i: (jnp.minimum(i, nblk - 1), 0)),
            const(1, D_MODEL),
            const(D_MODEL, IN_WIDTH),
            tab, tab, tab,
            const(1, LANES), const(1, LANES), const(CONV_K, CONV_WIDTH),
        ],
        out_specs=pl.BlockSpec((tm, D_MODEL), lambda i: (jnp.maximum(i - 2, 0), 0)),
        out_shape=jax.ShapeDtypeStruct((t, D_MODEL), BF16),
        scratch_shapes=[
            pltpu.VMEM((tm, D_MODEL), BF16),
            pltpu.VMEM((tm, D_MODEL), BF16),
            pltpu.VMEM((tm, IN_WIDTH), F32),
            pltpu.VMEM((tm, IN_WIDTH), F32),
            pltpu.VMEM((2, tm + QBLK, LANES), BF16),
            pltpu.VMEM((tm + QBLK, KV_WIDTH), BF16),
            pltpu.VMEM((tm + 8, CONV_WIDTH), F32),
        ],
        compiler_params=pltpu.CompilerParams(
            dimension_semantics=("arbitrary",),
            vmem_limit_bytes=58 * 1024 * 1024),
        name="front",
    )(sinks, x2, gain, w_in_p, *tabs, qg, kg, conv_w)


def _out_kernel(mix_ref, x_ref, p_ref, wo_ref, g2_ref, wg_ref, bg_ref, wp_ref, g3_ref, o_ref):
    x1 = x_ref[...] + jnp.dot(mix_ref[...], wo_ref[...], preferred_element_type=F32)
    ms = jnp.mean(x1 * x1, axis=-1, keepdims=True)
    hn = (x1 * lax.rsqrt(ms + EPS) * g2_ref[...]).astype(BF16)
    gate = jax.nn.sigmoid(jnp.dot(hn, wg_ref[...], preferred_element_type=F32) + bg_ref[...])
    pe = jnp.dot(p_ref[...].astype(BF16), wp_ref[...], preferred_element_type=F32)
    e = pe * lax.rsqrt(jnp.mean(pe * pe, axis=-1, keepdims=True) + EPS) * g3_ref[...]
    o_ref[...] = x1 + gate * e


def _out_stage(mix2, x2, p2, w_out_p, g2, w_gate, b_gate, w_proj, g3, tm):
    t = x2.shape[0]
    row = lambda w: pl.BlockSpec((tm, w), lambda i: (i, 0))
    const = lambda r, c: pl.BlockSpec((r, c), lambda i: (0, 0), pipeline_mode=pl.Buffered(1))
    return pl.pallas_call(
        _out_kernel,
        grid=(t // tm,),
        in_specs=[
            row(D_MODEL), row(D_MODEL), row(PLE_DIM),
            const(D_MODEL, D_MODEL), const(1, D_MODEL),
            const(D_MODEL, D_MODEL), const(1, D_MODEL),
            const(PLE_DIM, D_MODEL), const(1, D_MODEL),
        ],
        out_specs=row(D_MODEL),
        out_shape=jax.ShapeDtypeStruct((t, D_MODEL), F32),
        compiler_params=pltpu.CompilerParams(
            dimension_semantics=("arbitrary",),
            vmem_limit_bytes=56 * 1024 * 1024),
        name="outproj_ple",
    )(mix2, x2, p2, w_out_p, g2, w_gate, b_gate, w_proj, g3)


def _rope_tables(s):
    half = ROT_DIM // 2
    inv_freq = jnp.power(jnp.float32(ROPE_THETA), -jnp.arange(half, dtype=F32) * 2.0 / ROT_DIM)
    ang = jnp.arange(s).astype(F32)[:, None] * inv_freq[None, :]
    cos, sin = jnp.cos(ang), jnp.sin(ang)
    pad = lambda n: jnp.zeros((s, n), F32)
    c64 = jnp.concatenate([cos, cos, jnp.ones((s, HEAD_DIM - ROT_DIM), F32)], axis=1)
    s1 = jnp.concatenate([-sin, pad(HEAD_DIM - half)], axis=1)
    s2 = jnp.concatenate([pad(half), sin, pad(HEAD_DIM - ROT_DIM)], axis=1)
    return tuple(jnp.tile(t, (1, LANES // HEAD_DIM)) for t in (c64, s1, s2))


def _layer(x, p_i, norm_gain, w_in, q_gain, k_gain, sinks, conv_w, w_out,
           gate_norm_gain, w_gate, b_gate, w_proj, ple_norm_gain):
    b, s, _ = x.shape
    t = b * s
    sq, sk, sv, sg, sb, sc, sh = np.cumsum(
        [ATTN_WIDTH, KV_WIDTH, KV_WIDTH, ATTN_WIDTH, CONV_WIDTH, CONV_WIDTH, CONV_WIDTH])
    w_in_p = jnp.concatenate([
        _pair_perm_cols(w_in[:, :sq]), _pair_perm_cols(w_in[:, sv:sg]),
        w_in[:, sg:sb], w_in[:, sb:sc], w_in[:, sc:sh], w_in[:, sh:],
        w_in[:, sq:sk], w_in[:, sk:sv]], axis=1).astype(BF16)
    w_out_p = jnp.concatenate([
        _pair_perm_cols(w_out[:ATTN_WIDTH].T).T, w_out[ATTN_WIDTH:]], axis=0).astype(BF16)

    x2 = x.reshape(t, D_MODEL)
    qg = jnp.tile(q_gain * (HEAD_DIM ** -0.5), LANES // HEAD_DIM).reshape(1, LANES)
    kg = jnp.tile(k_gain, LANES // HEAD_DIM).reshape(1, LANES)
    mix = _front(x2, norm_gain.reshape(1, D_MODEL), w_in_p, sinks, _rope_tables(s), qg, kg, conv_w,
                 s, min(FRONT_TM, s))

    tm3 = min(512, t)
    out = _out_stage(mix, x2, p_i.reshape(t, PLE_DIM), w_out_p,
                     gate_norm_gain.reshape(1, D_MODEL), w_gate.astype(BF16),
                     b_gate.reshape(1, D_MODEL), w_proj.astype(BF16),
                     ple_norm_gain.reshape(1, D_MODEL), tm3)
    return out.reshape(b, s, D_MODEL)


def kernel(x, p, norm_gain, w_in, q_norm_gain, k_norm_gain, attn_sinks, conv_w, w_out,
           ple_gate_norm_gain, w_ple_gate, b_ple_gate, w_ple_proj, ple_norm_gain):
    for i in range(p.shape[0]):
        x = _layer(x, p[i], norm_gain[i], w_in[i], q_norm_gain[i], k_norm_gain[i], attn_sinks[i],
                   conv_w[i], w_out[i], ple_gate_norm_gain[i], w_ple_gate[i], b_ple_gate[i],
                   w_ple_proj[i], ple_norm_gain[i])
    return x
```

```python
import functools
import itertools

import jax
import jax.numpy as jnp
import numpy as np
from jax import lax
from jax.experimental import pallas as pl
from jax.experimental.pallas import tpu as pltpu

D_MODEL = 2048
PLE_DIM = 256
ATTN_WIDTH = 1024
CONV_WIDTH = 1024
HEAD_DIM = 64
N_Q_HEADS = 16
N_KV_HEADS = 4
KV_WIDTH = 256
WINDOW = 128
ROT_DIM = 16
ROPE_THETA = 500000.0
CONV_K = 3
EPS = 1e-6
NEG_INF = -1e30
IN_WIDTH = 2 * ATTN_WIDTH + 2 * KV_WIDTH + 4 * CONV_WIDTH

LANES = 128
PAIRS = ATTN_WIDTH // LANES
QBLK = 128

COL_Q, COL_GA, COL_B, COL_C, COL_H, COL_GC, COL_K, COL_V = 0, 1024, 2048, 3072, 4096, 5120, 6144, 6400
PROJ_CHUNK = 512
FRONT_TM = 256
FILL_PIECES = 4

F32 = jnp.float32
BF16 = jnp.bfloat16


def _pair_perm_cols(w):
    lead = w.shape[:-1]
    w = w.reshape(*lead, 2, 2, 4, HEAD_DIM)
    w = jnp.swapaxes(w, -3, -2)
    return w.reshape(*lead, ATTN_WIDTH)


def _mixer_steps(sinks_ref, z_ref, tabs, qg, kg, mix_ref, kband, vband, first_of_seq, tm):
    lane = lax.broadcasted_iota(jnp.int32, (1, LANES), 1)
    lo = lane < HEAD_DIM
    rr = (lax.broadcasted_iota(jnp.int32, (2 * LANES, LANES), 0) % LANES) // HEAD_DIM
    cc = lax.broadcasted_iota(jnp.int32, (2 * LANES, LANES), 1) // HEAD_DIM
    bd = jnp.where(rr == cc, 1.0, 0.0).astype(BF16)

    def head_sumsq(t):
        sq = t * t
        hi = sq.astype(BF16)
        rest = (sq - hi.astype(F32)).astype(BF16)
        return jnp.dot(jnp.concatenate([hi, rest], axis=1), bd, preferred_element_type=F32)

    def head_norm_rope(t, ss, gain, tab):
        rc, rs1, rs2 = tab
        y = t * lax.rsqrt(ss * (1.0 / HEAD_DIM) + EPS) * gain
        return y * rc + pltpu.roll(y, LANES - ROT_DIM // 2, 1) * rs1 + pltpu.roll(y, ROT_DIM // 2, 1) * rs2

    zcol = lambda col, r0, r1: z_ref[r0:r1, col:col + LANES]
    qi = lax.broadcasted_iota(jnp.int32, (QBLK, 2 * QBLK), 0)
    ci = lax.broadcasted_iota(jnp.int32, (QBLK, 2 * QBLK), 1)
    band = (ci > qi) & (ci <= qi + QBLK)
    nt = (((1,), (1,)), ((), ()))

    for qb in range(tm // QBLK):
        r0, r1 = qb * QBLK, (qb + 1) * QBLK
        tab = tuple(t[r0:r1] for t in tabs)
        mask = band & ((ci >= QBLK) | jnp.logical_not(first_of_seq)) if qb == 0 else band
        for a in range(2):
            if a == 0:
                k_raw = [zcol(COL_K + j * LANES, r0, r1) for j in range(2)]
                k_ss = [head_sumsq(t) for t in k_raw]
            q_raw = [zcol(COL_Q + p * LANES, r0, r1) for p in range(4 * a, 4 * a + 4)]
            q_ss = [head_sumsq(t) for t in q_raw]
            yield 1

            if a == 0:
                for j in range(2):
                    kband[j, QBLK + r0:QBLK + r1, :] = head_norm_rope(k_raw[j], k_ss[j], kg, tab).astype(BF16)
                vband[QBLK + r0:QBLK + r1, :] = z_ref[r0:r1, COL_V:COL_V + KV_WIDTH].astype(BF16)
            kt = kband[a, r0:r0 + 2 * QBLK, :]
            zero = jnp.zeros_like(kt)
            k_lo = jnp.where(lo, kt, zero)
            k_hi = jnp.where(lo, zero, kt)
            vt = vband[r0:r0 + 2 * QBLK, a * LANES:(a + 1) * LANES]
            v2 = jnp.concatenate([jnp.where(lo, vt, zero), jnp.where(lo, zero, vt)], axis=0)
            q = jnp.concatenate(
                [head_norm_rope(t, ss, qg, tab).astype(BF16) for t, ss in zip(q_raw, q_ss)], axis=0)
            s_lo = lax.dot_general(q, k_lo, nt, preferred_element_type=F32)
            s_hi = lax.dot_general(q, k_hi, nt, preferred_element_type=F32)
            yield 2

            for c in range(4):
                p = 4 * a + c
                es, inv = [], []
                for half, sc_all in ((0, s_lo), (1, s_hi)):
                    sink = sinks_ref[8 * a + 4 * half + c]
                    sc = jnp.where(mask, sc_all[c * QBLK:(c + 1) * QBLK], NEG_INF)
                    m = jnp.maximum(jnp.max(sc, axis=-1, keepdims=True), sink)
                    e = jnp.exp(sc - m)
                    den = jnp.sum(e, axis=-1, keepdims=True) + jnp.exp(sink - m)
                    es.append(e.astype(BF16))
                    inv.append(1.0 / den)
                o = jnp.dot(jnp.concatenate(es, axis=1), v2, preferred_element_type=F32)
                o = o * jnp.where(lo, inv[0], inv[1])
                g = zcol(COL_GA + p * LANES, r0, r1)
                mix_ref[r0:r1, p * LANES:(p + 1) * LANES] = (o * (g * jax.nn.sigmoid(g))).astype(BF16)

    kband[:, 0:QBLK, :] = kband[:, tm:tm + QBLK, :]
    vband[0:QBLK, :] = vband[tm:tm + QBLK, :]


def _conv_piece(z_ref, cw, mix_ref, u_scr, tm, c0, c1):
    seg = lambda col: z_ref[:, col + c0:col + c1]
    u = seg(COL_C) * seg(COL_H)
    u_scr[8:8 + tm, c0:c1] = u
    conv = (cw[0:1, c0:c1] * u_scr[6:6 + tm, c0:c1] + cw[1:2, c0:c1] * u_scr[7:7 + tm, c0:c1]
            + cw[2:3, c0:c1] * u)
    gc = seg(COL_GC)
    mix_ref[:, ATTN_WIDTH + c0:ATTN_WIDTH + c1] = (seg(COL_B) * conv * (gc * jax.nn.sigmoid(gc))).astype(BF16)
    u_scr[0:8, c0:c1] = u_scr[tm:tm + 8, c0:c1]


def _front_kernel(sinks_ref, x_ref, g_ref, w_ref, rc_ref, rs1_ref, rs2_ref, qg_ref, kg_ref, cw_ref,
                  mix_ref, h_buf, z_buf, kband, vband, u_scr, *, tm, blocks_per_seq):
    i = pl.program_id(0)
    first_of_seq = (i + 2 * blocks_per_seq - 2) % blocks_per_seq == 0
    slot = i % 2
    h_w, h_r = h_buf.at[slot], h_buf.at[1 - slot]
    z_w, z_r = z_buf.at[1 - slot], z_buf.at[slot]

    @pl.when(i == 0)
    def _():
        h_buf[1] = jnp.zeros(h_buf.shape[1:], BF16)
        z_buf[0] = jnp.zeros(z_buf.shape[1:], F32)
        kband[...] = jnp.zeros_like(kband)
        vband[...] = jnp.zeros_like(vband)
        u_scr[...] = jnp.zeros_like(u_scr)

    @pl.when(first_of_seq)
    def _():
        kband[:, 0:QBLK, :] = jnp.zeros((2, QBLK, LANES), BF16)
        vband[0:QBLK, :] = jnp.zeros((QBLK, KV_WIDTH), BF16)
        u_scr[0:8, :] = jnp.zeros((8, CONV_WIDTH), F32)

    def norm_piece(r0, r1):
        x = x_ref[r0:r1, :]
        ms = jnp.mean(x * x, axis=-1, keepdims=True)
        h_w[r0:r1, :] = (x * lax.rsqrt(ms + EPS) * g_ref[...]).astype(BF16)

    fillers = []
    for j in range(FILL_PIECES):
        cw_ = CONV_WIDTH // FILL_PIECES
        rw_ = tm // FILL_PIECES
        fillers.append(functools.partial(_conv_piece, z_r, cw_ref[...], mix_ref, u_scr, tm, j * cw_, (j + 1) * cw_))
        fillers.append(functools.partial(norm_piece, j * rw_, (j + 1) * rw_))
    fillers = iter(fillers)
    chunks = iter(range(IN_WIDTH // PROJ_CHUNK))

    def project(count):
        for n in itertools.islice(chunks, count):
            cols = slice(n * PROJ_CHUNK, (n + 1) * PROJ_CHUNK)
            z_w[:, cols] = jnp.dot(h_r[...], w_ref[:, cols], preferred_element_type=F32)

    for count in _mixer_steps(sinks_ref, z_r, (rc_ref[...], rs1_ref[...], rs2_ref[...]), qg_ref[...],
                              kg_ref[...], mix_ref, kband, vband, first_of_seq, tm):
        project(count)
        for fill in itertools.islice(fillers, 1):
            fill()
    project(None)
    for fill in fillers:
        fill()


def _front(x2, gain, w_in_p, sinks, tabs, qg, kg, conv_w, seq, tm):
    t = x2.shape[0]
    nblk = t // tm
    bps = seq // tm
    const = lambda r, c: pl.BlockSpec((r, c), lambda i: (0, 0), pipeline_mode=pl.Buffered(1))
    tab = pl.BlockSpec((tm, LANES), lambda i: ((i + 2 * bps - 2) % bps, 0))
    return pl.pallas_call(
        functools.partial(_front_kernel, tm=tm, blocks_per_seq=bps),
        grid=(nblk + 2,),
        in_specs=[
            pl.BlockSpec(memory_space=pltpu.SMEM),
            pl.BlockSpec((tm, D_MODEL), lambda i: (jnp.minimum(i, nblk - 1), 0)),
            const(1, D_MODEL),
            const(D_MODEL, IN_WIDTH),
            tab, tab, tab,
            const(1, LANES), const(1, LANES), const(CONV_K, CONV_WIDTH),
        ],
        out_specs=pl.BlockSpec((tm, D_MODEL), lambda i: (jnp.maximum(i - 2, 0), 0)),
        out_shape=jax.ShapeDtypeStruct((t, D_MODEL), BF16),
        scratch_shapes=[
            pltpu.VMEM((2, tm, D_MODEL), BF16),
            pltpu.VMEM((2, tm, IN_WIDTH), F32),
            pltpu.VMEM((2, tm + QBLK, LANES), BF16),
            pltpu.VMEM((tm + QBLK, KV_WIDTH), BF16),
            pltpu.VMEM((tm + 8, CONV_WIDTH), F32),
        ],
        compiler_params=pltpu.CompilerParams(
            dimension_semantics=("arbitrary",),
            vmem_limit_bytes=58 * 1024 * 1024),
        name="front",
    )(sinks, x2, gain, w_in_p, *tabs, qg, kg, conv_w)


def _out_kernel(mix_ref, x_ref, p_ref, wo_ref, g2_ref, wg_ref, bg_ref, wp_ref, g3_ref, o_ref):
    x1 = x_ref[...] + jnp.dot(mix_ref[...], wo_ref[...], preferred_element_type=F32)
    ms = jnp.mean(x1 * x1, axis=-1, keepdims=True)
    hn = (x1 * lax.rsqrt(ms + EPS) * g2_ref[...]).astype(BF16)
    gate = jax.nn.sigmoid(jnp.dot(hn, wg_ref[...], preferred_element_type=F32) + bg_ref[...])
    pe = jnp.dot(p_ref[...].astype(BF16), wp_ref[...], preferred_element_type=F32)
    e = pe * lax.rsqrt(jnp.mean(pe * pe, axis=-1, keepdims=True) + EPS) * g3_ref[...]
    o_ref[...] = x1 + gate * e


def _out_stage(mix2, x2, p2, w_out_p, g2, w_gate, b_gate, w_proj, g3, tm):
    t = x2.shape[0]
    row = lambda w: pl.BlockSpec((tm, w), lambda i: (i, 0))
    const = lambda r, c: pl.BlockSpec((r, c), lambda i: (0, 0), pipeline_mode=pl.Buffered(1))
    return pl.pallas_call(
        _out_kernel,
        grid=(t // tm,),
        in_specs=[
            row(D_MODEL), row(D_MODEL), row(PLE_DIM),
            const(D_MODEL, D_MODEL), const(1, D_MODEL),
            const(D_MODEL, D_MODEL), const(1, D_MODEL),
            const(PLE_DIM, D_MODEL), const(1, D_MODEL),
        ],
        out_specs=row(D_MODEL),
        out_shape=jax.ShapeDtypeStruct((t, D_MODEL), F32),
        compiler_params=pltpu.CompilerParams(
            dimension_semantics=("arbitrary",),
            vmem_limit_bytes=56 * 1024 * 1024),
        name="outproj_ple",
    )(mix2, x2, p2, w_out_p, g2, w_gate, b_gate, w_proj, g3)


def _rope_tables(s):
    half = ROT_DIM // 2
    inv_freq = jnp.power(jnp.float32(ROPE_THETA), -jnp.arange(half, dtype=F32) * 2.0 / ROT_DIM)
    ang = jnp.arange(s).astype(F32)[:, None] * inv_freq[None, :]
    cos, sin = jnp.cos(ang), jnp.sin(ang)
    pad = lambda n: jnp.zeros((s, n), F32)
    c64 = jnp.concatenate([cos, cos, jnp.ones((s, HEAD_DIM - ROT_DIM), F32)], axis=1)
    s1 = jnp.concatenate([-sin, pad(HEAD_DIM - half)], axis=1)
    s2 = jnp.concatenate([pad(half), sin, pad(HEAD_DIM - ROT_DIM)], axis=1)
    return tuple(jnp.tile(t, (1, LANES // HEAD_DIM)) for t in (c64, s1, s2))


def _layer(x, p_i, norm_gain, w_in, q_gain, k_gain, sinks, conv_w, w_out,
           gate_norm_gain, w_gate, b_gate, w_proj, ple_norm_gain):
    b, s, _ = x.shape
    t = b * s
    sq, sk, sv, sg, sb, sc, sh = np.cumsum(
        [ATTN_WIDTH, KV_WIDTH, KV_WIDTH, ATTN_WIDTH, CONV_WIDTH, CONV_WIDTH, CONV_WIDTH])
    w_in_p = jnp.concatenate([
        _pair_perm_cols(w_in[:, :sq]), _pair_perm_cols(w_in[:, sv:sg]),
        w_in[:, sg:sb], w_in[:, sb:sc], w_in[:, sc:sh], w_in[:, sh:],
        w_in[:, sq:sk], w_in[:, sk:sv]], axis=1).astype(BF16)
    w_out_p = jnp.concatenate([
        _pair_perm_cols(w_out[:ATTN_WIDTH].T).T, w_out[ATTN_WIDTH:]], axis=0).astype(BF16)

    x2 = x.reshape(t, D_MODEL)
    qg = jnp.tile(q_gain * (HEAD_DIM ** -0.5), LANES // HEAD_DIM).reshape(1, LANES)
    kg = jnp.tile(k_gain, LANES // HEAD_DIM).reshape(1, LANES)
    mix = _front(x2, norm_gain.reshape(1, D_MODEL), w_in_p, sinks, _rope_tables(s), qg, kg, conv_w,
                 s, min(FRONT_TM, s))

    tm3 = min(512, t)
    out = _out_stage(mix, x2, p_i.reshape(t, PLE_DIM), w_out_p,
                     gate_norm_gain.reshape(1, D_MODEL), w_gate.astype(BF16),
                     b_gate.reshape(1, D_MODEL), w_proj.astype(BF16),
                     ple_norm_gain.reshape(1, D_MODEL), tm3)
    return out.reshape(b, s, D_MODEL)


def kernel(x, p, norm_gain, w_in, q_norm_gain, k_norm_gain, attn_sinks, conv_w, w_out,
           ple_gate_norm_gain, w_ple_gate, b_ple_gate, w_ple_proj, ple_norm_gain):
    for i in range(p.shape[0]):
        x = _layer(x, p[i], norm_gain[i], w_in[i], q_norm_gain[i], k_norm_gain[i], attn_sinks[i],
                   conv_w[i], w_out[i], ple_gate_norm_gain[i], w_ple_gate[i], b_ple_gate[i],
                   w_ple_proj[i], ple_norm_gain[i])
    return x
```

```python
import functools
import itertools

import jax
import jax.numpy as jnp
from jax import lax
from jax.experimental import pallas as pl
from jax.experimental.pallas import tpu as pltpu

D_MODEL = 2048
PLE_DIM = 256
ATTN_WIDTH = 1024
CONV_WIDTH = 1024
HEAD_DIM = 64
N_Q_HEADS = 16
N_KV_HEADS = 4
KV_WIDTH = 256
WINDOW = 128
ROT_DIM = 16
ROPE_THETA = 500000.0
CONV_K = 3
EPS = 1e-6
NEG_INF = -1e30
IN_WIDTH = 2 * ATTN_WIDTH + 2 * KV_WIDTH + 4 * CONV_WIDTH

LANES = 128
QBLK = 128

COL_Q, COL_K, COL_V, COL_GA, COL_B, COL_C, COL_H, COL_GC = 0, 1024, 1280, 1536, 2560, 3584, 4608, 5632
PROJ_CHUNK = 512
FRONT_TM = 256
FILL_PIECES = 8
TAIL_CHUNKS = 3

F32 = jnp.float32
BF16 = jnp.bfloat16


class _Attention:
    def __init__(self, sinks_ref, z_ref, tabs, qg, kg, mix_ref, kband, vband, first_of_seq, tm):
        self.sinks_ref, self.z_ref, self.tabs, self.qg, self.kg = sinks_ref, z_ref, tabs, qg, kg
        self.mix_ref, self.kband, self.vband, self.tm = mix_ref, kband, vband, tm
        self.units = [(qb, g) for qb in range(tm // QBLK) for g in range(N_KV_HEADS)]
        self.lo = lax.broadcasted_iota(jnp.int32, (1, LANES), 1) < HEAD_DIM
        rr = (lax.broadcasted_iota(jnp.int32, (2 * LANES, LANES), 0) % LANES) // HEAD_DIM
        cc = lax.broadcasted_iota(jnp.int32, (2 * LANES, LANES), 1) // HEAD_DIM
        self.bd = jnp.where(rr == cc, 1.0, 0.0).astype(BF16)
        qi = lax.broadcasted_iota(jnp.int32, (QBLK, 2 * QBLK), 0)
        ci = lax.broadcasted_iota(jnp.int32, (QBLK, 2 * QBLK), 1)
        band = (ci > qi) & (ci <= qi + QBLK)
        self.masks = [band & ((ci >= QBLK) | jnp.logical_not(first_of_seq))] + [band] * (tm // QBLK - 1)
        self.state = {}

    def _zcol(self, col, qb):
        return self.z_ref[qb * QBLK:(qb + 1) * QBLK, col:col + LANES]

    def _head_sumsq(self, t):
        sq = t * t
        hi = sq.astype(BF16)
        rest = (sq - hi.astype(F32)).astype(BF16)
        return jnp.dot(jnp.concatenate([hi, rest], axis=1), self.bd, preferred_element_type=F32)

    def _head_norm_rope(self, t, ss, gain, qb):
        rc, rs1, rs2 = (tab[qb * QBLK:(qb + 1) * QBLK] for tab in self.tabs)
        y = t * lax.rsqrt(ss * (1.0 / HEAD_DIM) + EPS) * gain
        return y * rc + pltpu.roll(y, LANES - ROT_DIM // 2, 1) * rs1 + pltpu.roll(y, ROT_DIM // 2, 1) * rs2

    def _both_halves(self, t):
        tr = pltpu.roll(t, HEAD_DIM, 1)
        return jnp.where(self.lo, t, tr), jnp.where(self.lo, tr, t)

    def sumsq(self, u):
        qb, g = self.units[u]
        st = self.state[u] = {}
        if g == 0:
            st["k_raw"] = [self._zcol(COL_K + j * LANES, qb) for j in range(KV_WIDTH // LANES)]
            st["k_ss"] = [self._head_sumsq(t) for t in st["k_raw"]]
        st["q_raw"] = [self._zcol(COL_Q + (2 * g + c) * LANES, qb) for c in range(2)]
        st["q_ss"] = [self._head_sumsq(t) for t in st["q_raw"]]

    def scores(self, u):
        qb, g = self.units[u]
        st = self.state[u]
        rows = slice(QBLK + qb * QBLK, QBLK + (qb + 1) * QBLK)
        if g == 0:
            for j in range(KV_WIDTH // LANES):
                k = self._head_norm_rope(st["k_raw"][j], st["k_ss"][j], self.kg, qb)
                for half, t in enumerate(self._both_halves(k)):
                    self.kband[2 * j + half, rows, :] = t.astype(BF16)
                for half, t in enumerate(self._both_halves(self._zcol(COL_V + j * LANES, qb))):
                    self.vband[2 * j + half, rows, :] = t.astype(BF16)
        q = jnp.concatenate(
            [self._head_norm_rope(t, ss, self.qg, qb).astype(BF16) for t, ss in zip(st["q_raw"], st["q_ss"])],
            axis=0)
        kt = self.kband[g, qb * QBLK:(qb + 2) * QBLK, :]
        zero = jnp.zeros_like(kt)
        nt = (((1,), (1,)), ((), ()))
        st["s"] = [lax.dot_general(q, jnp.where(self.lo, kt, zero), nt, preferred_element_type=F32),
                   lax.dot_general(q, jnp.where(self.lo, zero, kt), nt, preferred_element_type=F32)]

    def output(self, u):
        qb, g = self.units[u]
        st = self.state.pop(u)
        vt = self.vband[g, qb * QBLK:(qb + 2) * QBLK, :]
        zero = jnp.zeros_like(vt)
        v2 = jnp.concatenate([jnp.where(self.lo, vt, zero), jnp.where(self.lo, zero, vt)], axis=0)
        for c in range(2):
            p = 2 * g + c
            es, inv = [], []
            for half in range(2):
                sink = self.sinks_ref[2 * p + half]
                sc = jnp.where(self.masks[qb], st["s"][half][c * QBLK:(c + 1) * QBLK], NEG_INF)
                m = jnp.maximum(jnp.max(sc, axis=-1, keepdims=True), sink)
                e = jnp.exp(sc - m)
                den = jnp.sum(e, axis=-1, keepdims=True) + jnp.exp(sink - m)
                es.append(e.astype(BF16))
                inv.append(1.0 / den)
            o = jnp.dot(jnp.concatenate(es, axis=1), v2, preferred_element_type=F32)
            o = o * jnp.where(self.lo, inv[0], inv[1])
            gate = self._zcol(COL_GA + p * LANES, qb)
            self.mix_ref[qb * QBLK:(qb + 1) * QBLK, p * LANES:(p + 1) * LANES] = (
                o * (gate * jax.nn.sigmoid(gate))).astype(BF16)

    def carry(self):
        self.kband[:, 0:QBLK, :] = self.kband[:, self.tm:self.tm + QBLK, :]
        self.vband[:, 0:QBLK, :] = self.vband[:, self.tm:self.tm + QBLK, :]


def _conv_piece(z_ref, cw, mix_ref, u_scr, tm, c0, c1):
    seg = lambda col: z_ref[:, col + c0:col + c1]
    u = seg(COL_C) * seg(COL_H)
    u_scr[8:8 + tm, c0:c1] = u
    conv = (cw[0:1, c0:c1] * u_scr[6:6 + tm, c0:c1] + cw[1:2, c0:c1] * u_scr[7:7 + tm, c0:c1]
            + cw[2:3, c0:c1] * u)
    gc = seg(COL_GC)
    y = seg(COL_B) * conv * (gc * jax.nn.sigmoid(gc))
    mix_ref[:, ATTN_WIDTH + c0:ATTN_WIDTH + c1] = y.astype(BF16)
    u_scr[0:8, c0:c1] = u_scr[tm:tm + 8, c0:c1]


def _front_kernel(sinks_ref, x_ref, g_ref, w_ref, rc_ref, rs1_ref, rs2_ref, qg_ref, kg_ref, cw_ref,
                  mix_ref, h0, h1, z0, z1, kband, vband, u_scr, *, tm, blocks_per_seq):
    i = pl.program_id(0)
    first_of_seq = (i + 2 * blocks_per_seq - 2) % blocks_per_seq == 0

    @pl.when(i == 0)
    def _():
        h1[...] = jnp.zeros_like(h1)
        z0[...] = jnp.zeros_like(z0)
        kband[...] = jnp.zeros_like(kband)
        vband[...] = jnp.zeros_like(vband)
        u_scr[...] = jnp.zeros_like(u_scr)

    @pl.when(first_of_seq)
    def _():
        kband[:, 0:QBLK, :] = jnp.zeros((N_KV_HEADS, QBLK, LANES), BF16)
        vband[:, 0:QBLK, :] = jnp.zeros((N_KV_HEADS, QBLK, LANES), BF16)
        u_scr[0:8, :] = jnp.zeros((8, CONV_WIDTH), F32)

    def body(h_w, h_r, z_w, z_r):
        def norm_piece(r0, r1):
            x = x_ref[r0:r1, :]
            ms = jnp.mean(x * x, axis=-1, keepdims=True)
            h_w[r0:r1, :] = (x * lax.rsqrt(ms + EPS) * g_ref[...]).astype(BF16)

        fillers = []
        cw_, rw_ = CONV_WIDTH // FILL_PIECES, tm // FILL_PIECES
        for j in range(FILL_PIECES):
            fillers.append(functools.partial(_conv_piece, z_r, cw_ref[...], mix_ref, u_scr, tm, j * cw_, (j + 1) * cw_))
            fillers.append(functools.partial(norm_piece, j * rw_, (j + 1) * rw_))
        fillers = iter(fillers)
        chunks = iter(range(IN_WIDTH // PROJ_CHUNK))

        def project(count):
            for n in itertools.islice(chunks, count):
                cols = slice(n * PROJ_CHUNK, (n + 1) * PROJ_CHUNK)
                z_w[:, cols] = jnp.dot(h_r[...], w_ref[:, cols], preferred_element_type=F32)

        def fill(count):
            for f in itertools.islice(fillers, count):
                f()

        att = _Attention(sinks_ref, z_r, (rc_ref[...], rs1_ref[...], rs2_ref[...]), qg_ref[...], kg_ref[...],
                         mix_ref, kband, vband, first_of_seq, tm)
        n_units = len(att.units)
        att.sumsq(0)
        project(1)
        fill(1)
        for u in range(n_units):
            if u > 0:
                att.output(u - 1)
            att.scores(u)
            if u + 1 < n_units:
                att.sumsq(u + 1)
            project(1)
            fill(1)
        att.output(n_units - 1)
        att.carry()
        project(1)
        fill(1)
        for _ in range(TAIL_CHUNKS):
            fill(2)
            project(1)

    @pl.when(i % 2 == 0)
    def _():
        body(h0, h1, z1, z0)

    @pl.when(i % 2 == 1)
    def _():
        body(h1, h0, z0, z1)


def _front(x2, gain, w_in, sinks, tabs, qg, kg, conv_w, seq, tm):
    t = x2.shape[0]
    nblk = t // tm
    bps = seq // tm
    const = lambda r, c: pl.BlockSpec((r, c), lambda i: (0, 0), pipeline_mode=pl.Buffered(1))
    tab = pl.BlockSpec((tm, LANES), lambda i: ((i + 2 * bps - 2) % bps, 0))
    return pl.pallas_call(
        functools.partial(_front_kernel, tm=tm, blocks_per_seq=bps),
        grid=(nblk + 2,),
        in_specs=[
            pl.BlockSpec(memory_space=pltpu.SMEM),
            pl.BlockSpec((tm, D_MODEL), lambda i: (jnp.minimum(i, nblk - 1), 0)),
            const(1, D_MODEL),
            const(D_MODEL, IN_WIDTH),
            tab, tab, tab,
            const(1, LANES), const(1, LANES), const(CONV_K, CONV_WIDTH),
        ],
        out_specs=pl.BlockSpec((tm, D_MODEL), lambda i: (jnp.maximum(i - 2, 0), 0)),
        out_shape=jax.ShapeDtypeStruct((t, D_MODEL), BF16),
        scratch_shapes=[
            pltpu.VMEM((tm, D_MODEL), BF16),
            pltpu.VMEM((tm, D_MODEL), BF16),
            pltpu.VMEM((tm, IN_WIDTH), F32),
            pltpu.VMEM((tm, IN_WIDTH), F32),
            pltpu.VMEM((N_KV_HEADS, tm + QBLK, LANES), BF16),
            pltpu.VMEM((N_KV_HEADS, tm + QBLK, LANES), BF16),
            pltpu.VMEM((tm + 8, CONV_WIDTH), F32),
        ],
        compiler_params=pltpu.CompilerParams(
            dimension_semantics=("arbitrary",),
            vmem_limit_bytes=58 * 1024 * 1024),
        name="front",
    )(sinks, x2, gain, w_in, *tabs, qg, kg, conv_w)


def _out_kernel(mix_ref, x_ref, p_ref, wo_ref, g2_ref, wg_ref, bg_ref, wp_ref, g3_ref, o_ref):
    tm = o_ref.shape[0]
    halves = [slice(r * (tm // 2), (r + 1) * (tm // 2)) for r in range(2)]
    for rows in halves:
        o_ref[rows, :] = x_ref[rows, :] + jnp.dot(mix_ref[rows, :], wo_ref[...], preferred_element_type=F32)
    for rows in halves:
        x1 = o_ref[rows, :]
        ms = jnp.mean(x1 * x1, axis=-1, keepdims=True)
        hn = (x1 * lax.rsqrt(ms + EPS) * g2_ref[...]).astype(BF16)
        gate = jax.nn.sigmoid(jnp.dot(hn, wg_ref[...], preferred_element_type=F32) + bg_ref[...])
        pe = jnp.dot(p_ref[rows, :].astype(BF16), wp_ref[...], preferred_element_type=F32)
        e = pe * lax.rsqrt(jnp.mean(pe * pe, axis=-1, keepdims=True) + EPS) * g3_ref[...]
        o_ref[rows, :] = x1 + gate * e


def _out_stage(mix2, x2, p2, w_out, g2, w_gate, b_gate, w_proj, g3, tm):
    t = x2.shape[0]
    row = lambda w: pl.BlockSpec((tm, w), lambda i: (i, 0))
    const = lambda r, c: pl.BlockSpec((r, c), lambda i: (0, 0), pipeline_mode=pl.Buffered(1))
    return pl.pallas_call(
        _out_kernel,
        grid=(t // tm,),
        in_specs=[
            row(D_MODEL), row(D_MODEL), row(PLE_DIM),
            const(D_MODEL, D_MODEL), const(1, D_MODEL),
            const(D_MODEL, D_MODEL), const(1, D_MODEL),
            const(PLE_DIM, D_MODEL), const(1, D_MODEL),
        ],
        out_specs=row(D_MODEL),
        out_shape=jax.ShapeDtypeStruct((t, D_MODEL), F32),
        compiler_params=pltpu.CompilerParams(
            dimension_semantics=("arbitrary",),
            vmem_limit_bytes=56 * 1024 * 1024),
        name="outproj_ple",
    )(mix2, x2, p2, w_out, g2, w_gate, b_gate, w_proj, g3)


def _rope_tables(s):
    half = ROT_DIM // 2
    inv_freq = jnp.power(jnp.float32(ROPE_THETA), -jnp.arange(half, dtype=F32) * 2.0 / ROT_DIM)
    ang = jnp.arange(s).astype(F32)[:, None] * inv_freq[None, :]
    cos, sin = jnp.cos(ang), jnp.sin(ang)
    pad = lambda n: jnp.zeros((s, n), F32)
    c64 = jnp.concatenate([cos, cos, jnp.ones((s, HEAD_DIM - ROT_DIM), F32)], axis=1)
    s1 = jnp.concatenate([-sin, pad(HEAD_DIM - half)], axis=1)
    s2 = jnp.concatenate([pad(half), sin, pad(HEAD_DIM - ROT_DIM)], axis=1)
    return tuple(jnp.tile(t, (1, LANES // HEAD_DIM)) for t in (c64, s1, s2))


def _layer(x, p_i, norm_gain, w_in, q_gain, k_gain, sinks, conv_w, w_out,
           gate_norm_gain, w_gate, b_gate, w_proj, ple_norm_gain):
    b, s, _ = x.shape
    t = b * s
    x2 = x.reshape(t, D_MODEL)
    qg = jnp.tile(q_gain * (HEAD_DIM ** -0.5), LANES // HEAD_DIM).reshape(1, LANES)
    kg = jnp.tile(k_gain, LANES // HEAD_DIM).reshape(1, LANES)
    mix = _front(x2, norm_gain.reshape(1, D_MODEL), w_in.astype(BF16), sinks, _rope_tables(s), qg, kg,
                 conv_w, s, min(FRONT_TM, s))

    tm3 = min(512, t)
    out = _out_stage(mix, x2, p_i.reshape(t, PLE_DIM), w_out.astype(BF16),
                     gate_norm_gain.reshape(1, D_MODEL), w_gate.astype(BF16),
                     b_gate.reshape(1, D_MODEL), w_proj.astype(BF16),
                     ple_norm_gain.reshape(1, D_MODEL), tm3)
    return out.reshape(b, s, D_MODEL)


def kernel(x, p, norm_gain, w_in, q_norm_gain, k_norm_gain, attn_sinks, conv_w, w_out,
           ple_gate_norm_gain, w_ple_gate, b_ple_gate, w_ple_proj, ple_norm_gain):
    for i in range(p.shape[0]):
        x = _layer(x, p[i], norm_gain[i], w_in[i], q_norm_gain[i], k_norm_gain[i], attn_sinks[i],
                   conv_w[i], w_out[i], ple_gate_norm_gain[i], w_ple_gate[i], b_ple_gate[i],
                   w_ple_proj[i], ple_norm_gain[i])
    return x
```

```python
import functools
import itertools

import jax
import jax.numpy as jnp
from jax import lax
from jax.experimental import pallas as pl
from jax.experimental.pallas import tpu as pltpu

D_MODEL = 2048
PLE_DIM = 256
ATTN_WIDTH = 1024
CONV_WIDTH = 1024
HEAD_DIM = 64
N_Q_HEADS = 16
N_KV_HEADS = 4
KV_WIDTH = 256
WINDOW = 128
ROT_DIM = 16
ROPE_THETA = 500000.0
CONV_K = 3
EPS = 1e-6
NEG_INF = -1e30
IN_WIDTH = 2 * ATTN_WIDTH + 2 * KV_WIDTH + 4 * CONV_WIDTH

LANES = 128
QBLK = 128

COL_Q, COL_K, COL_V, COL_GA, COL_B, COL_C, COL_H, COL_GC = 0, 1024, 1280, 1536, 2560, 3584, 4608, 5632
ATTN_COLS = COL_B
ZC_B, ZC_C, ZC_H, ZC_GC = (c - COL_B for c in (COL_B, COL_C, COL_H, COL_GC))
PROJ_CHUNK = 512
FRONT_TM = 256
FILL_PIECES = 8
TAIL_CHUNKS = 3
OUT_TM = 512

F32 = jnp.float32
BF16 = jnp.bfloat16


class _Attention:
    def __init__(self, sinks_ref, z_ref, tabs, qg, kg, mix_ref, kband, vband, first_of_seq, tm):
        self.sinks_ref, self.z_ref, self.tabs, self.qg, self.kg = sinks_ref, z_ref, tabs, qg, kg
        self.mix_ref, self.kband, self.vband, self.tm = mix_ref, kband, vband, tm
        self.units = [(qb, g) for qb in range(tm // QBLK) for g in range(N_KV_HEADS)]
        self.lo = lax.broadcasted_iota(jnp.int32, (1, LANES), 1) < HEAD_DIM
        rr = (lax.broadcasted_iota(jnp.int32, (2 * LANES, LANES), 0) % LANES) // HEAD_DIM
        cc = lax.broadcasted_iota(jnp.int32, (2 * LANES, LANES), 1) // HEAD_DIM
        self.bd = jnp.where(rr == cc, 1.0, 0.0).astype(BF16)
        qi = lax.broadcasted_iota(jnp.int32, (QBLK, 2 * QBLK), 0)
        ci = lax.broadcasted_iota(jnp.int32, (QBLK, 2 * QBLK), 1)
        band = (ci > qi) & (ci <= qi + QBLK)
        self.masks = [band & ((ci >= QBLK) | jnp.logical_not(first_of_seq))] + [band] * (tm // QBLK - 1)
        self.state = {}

    def _zcol(self, col, qb):
        return self.z_ref[qb * QBLK:(qb + 1) * QBLK, col:col + LANES]

    def _head_sumsq(self, t):
        sq = t * t
        hi = sq.astype(BF16)
        rest = (sq - hi.astype(F32)).astype(BF16)
        return jnp.dot(jnp.concatenate([hi, rest], axis=1), self.bd, preferred_element_type=F32)

    def _head_norm_rope(self, t, ss, gain, qb):
        rc, rs1, rs2 = (tab[qb * QBLK:(qb + 1) * QBLK] for tab in self.tabs)
        y = t * lax.rsqrt(ss * (1.0 / HEAD_DIM) + EPS) * gain
        return y * rc + pltpu.roll(y, LANES - ROT_DIM // 2, 1) * rs1 + pltpu.roll(y, ROT_DIM // 2, 1) * rs2

    def _both_halves(self, t):
        tr = pltpu.roll(t, HEAD_DIM, 1)
        return jnp.where(self.lo, t, tr), jnp.where(self.lo, tr, t)

    def sumsq(self, u):
        qb, g = self.units[u]
        st = self.state[u] = {}
        if g == 0:
            st["k_raw"] = [self._zcol(COL_K + j * LANES, qb) for j in range(KV_WIDTH // LANES)]
            st["k_ss"] = [self._head_sumsq(t) for t in st["k_raw"]]
        st["q_raw"] = [self._zcol(COL_Q + (2 * g + c) * LANES, qb) for c in range(2)]
        st["q_ss"] = [self._head_sumsq(t) for t in st["q_raw"]]

    def scores(self, u):
        qb, g = self.units[u]
        st = self.state[u]
        rows = slice(QBLK + qb * QBLK, QBLK + (qb + 1) * QBLK)
        if g == 0:
            for j in range(KV_WIDTH // LANES):
                kt = self._head_norm_rope(st["k_raw"][j], st["k_ss"][j], self.kg, qb).T
                for half in range(2):
                    self.kband[2 * j + half, :, rows] = kt[half * HEAD_DIM:(half + 1) * HEAD_DIM].astype(BF16)
                for half, t in enumerate(self._both_halves(self._zcol(COL_V + j * LANES, qb))):
                    self.vband[2 * j + half, rows, :] = t.astype(BF16)
        q = jnp.concatenate(
            [self._head_norm_rope(t, ss, self.qg, qb).astype(BF16) for t, ss in zip(st["q_raw"], st["q_ss"])],
            axis=0)
        kt = self.kband[g, :, qb * QBLK:(qb + 2) * QBLK]
        zero = jnp.zeros_like(kt)
        st["s"] = [jnp.dot(q, jnp.concatenate([kt, zero], axis=0), preferred_element_type=F32),
                   jnp.dot(q, jnp.concatenate([zero, kt], axis=0), preferred_element_type=F32)]

    def output(self, u):
        qb, g = self.units[u]
        st = self.state.pop(u)
        vt = self.vband[g, qb * QBLK:(qb + 2) * QBLK, :]
        zero = jnp.zeros_like(vt)
        v2 = jnp.concatenate([jnp.where(self.lo, vt, zero), jnp.where(self.lo, zero, vt)], axis=0)
        for c in range(2):
            p = 2 * g + c
            es, inv = [], []
            for half in range(2):
                sink = self.sinks_ref[2 * p + half]
                sc = jnp.where(self.masks[qb], st["s"][half][c * QBLK:(c + 1) * QBLK], NEG_INF)
                m = jnp.maximum(jnp.max(sc, axis=-1, keepdims=True), sink)
                e = jnp.exp(sc - m)
                den = jnp.sum(e, axis=-1, keepdims=True) + jnp.exp(sink - m)
                es.append(e.astype(BF16))
                inv.append(1.0 / den)
            o = jnp.dot(jnp.concatenate(es, axis=1), v2, preferred_element_type=F32)
            o = o * jnp.where(self.lo, inv[0], inv[1])
            gate = self._zcol(COL_GA + p * LANES, qb)
            self.mix_ref[qb * QBLK:(qb + 1) * QBLK, p * LANES:(p + 1) * LANES] = (
                o * (gate * jax.nn.sigmoid(gate))).astype(BF16)

    def carry(self):
        self.kband[:, :, 0:QBLK] = self.kband[:, :, self.tm:self.tm + QBLK]
        self.vband[:, 0:QBLK, :] = self.vband[:, self.tm:self.tm + QBLK, :]


def _gated_conv(zc_ref, cw, u_scr, rows, n_rows):
    seg = lambda col: zc_ref[rows, col:col + CONV_WIDTH]
    u = seg(ZC_C) * seg(ZC_H)
    r0 = 8 + rows.start
    u_scr[r0:r0 + n_rows, :] = u
    conv = cw[0:1] * u_scr[r0 - 2:r0 - 2 + n_rows, :] + cw[1:2] * u_scr[r0 - 1:r0 - 1 + n_rows, :] + cw[2:3] * u
    gc = seg(ZC_GC)
    return seg(ZC_B) * conv * (gc * jax.nn.sigmoid(gc))


def _front_kernel(sinks_ref, x_ref, g_ref, w_ref, rc_ref, rs1_ref, rs2_ref, qg_ref, kg_ref,
                  mix_ref, zc_ref, h0, h1, z0, z1, kband, vband, *, tm, blocks_per_seq):
    i = pl.program_id(0)
    first_of_seq = (i + 2 * blocks_per_seq - 2) % blocks_per_seq == 0

    @pl.when(i == 0)
    def _():
        h1[...] = jnp.zeros_like(h1)
        z0[...] = jnp.zeros_like(z0)
        kband[...] = jnp.zeros_like(kband)
        vband[...] = jnp.zeros_like(vband)

    @pl.when(first_of_seq)
    def _():
        kband[:, :, 0:QBLK] = jnp.zeros((N_KV_HEADS, HEAD_DIM, QBLK), BF16)
        vband[:, 0:QBLK, :] = jnp.zeros((N_KV_HEADS, QBLK, LANES), BF16)

    def body(h_w, h_r, z_w, z_r):
        def norm_piece(r0, r1):
            x = x_ref[r0:r1, :]
            ms = jnp.mean(x * x, axis=-1, keepdims=True)
            h_w[r0:r1, :] = (x * lax.rsqrt(ms + EPS) * g_ref[...]).astype(BF16)

        rw_ = tm // FILL_PIECES
        fillers = iter([functools.partial(norm_piece, j * rw_, (j + 1) * rw_) for j in range(FILL_PIECES)])
        chunks = iter(range(IN_WIDTH // PROJ_CHUNK))

        def project(count):
            for n in itertools.islice(chunks, count):
                res = jnp.dot(h_r[...], w_ref[:, n * PROJ_CHUNK:(n + 1) * PROJ_CHUNK], preferred_element_type=F32)
                if n < ATTN_COLS // PROJ_CHUNK:
                    z_w[:, n * PROJ_CHUNK:(n + 1) * PROJ_CHUNK] = res
                else:
                    zc_ref[:, n * PROJ_CHUNK - ATTN_COLS:(n + 1) * PROJ_CHUNK - ATTN_COLS] = res

        def fill(count):
            for f in itertools.islice(fillers, count):
                f()

        att = _Attention(sinks_ref, z_r, (rc_ref[...], rs1_ref[...], rs2_ref[...]), qg_ref[...], kg_ref[...],
                         mix_ref, kband, vband, first_of_seq, tm)
        n_units = len(att.units)
        att.sumsq(0)
        project(1)
        for u in range(n_units):
            if u > 0:
                att.output(u - 1)
            att.scores(u)
            if u + 1 < n_units:
                att.sumsq(u + 1)
            project(1)
        att.output(n_units - 1)
        att.carry()
        project(1)
        for _ in range(TAIL_CHUNKS):
            project(1)
        fill(FILL_PIECES)

    @pl.when(i % 2 == 0)
    def _():
        body(h0, h1, z1, z0)

    @pl.when(i % 2 == 1)
    def _():
        body(h1, h0, z0, z1)


def _front(x2, gain, w_in, sinks, tabs, qg, kg, seq, tm):
    t = x2.shape[0]
    nblk = t // tm
    bps = seq // tm
    const = lambda r, c: pl.BlockSpec((r, c), lambda i: (0, 0), pipeline_mode=pl.Buffered(1))
    tab = pl.BlockSpec((tm, LANES), lambda i: ((i + 2 * bps - 2) % bps, 0))
    return pl.pallas_call(
        functools.partial(_front_kernel, tm=tm, blocks_per_seq=bps),
        grid=(nblk + 2,),
        in_specs=[
            pl.BlockSpec(memory_space=pltpu.SMEM),
            pl.BlockSpec((tm, D_MODEL), lambda i: (jnp.minimum(i, nblk - 1), 0)),
            const(1, D_MODEL),
            const(D_MODEL, IN_WIDTH),
            tab, tab, tab,
            const(1, LANES), const(1, LANES),
        ],
        out_specs=[
            pl.BlockSpec((tm, ATTN_WIDTH), lambda i: (jnp.maximum(i - 2, 0), 0)),
            pl.BlockSpec((tm, IN_WIDTH - ATTN_COLS), lambda i: (jnp.clip(i - 1, 0, nblk - 1), 0)),
        ],
        out_shape=[
            jax.ShapeDtypeStruct((t, ATTN_WIDTH), BF16),
            jax.ShapeDtypeStruct((t, IN_WIDTH - ATTN_COLS), F32),
        ],
        scratch_shapes=[
            pltpu.VMEM((tm, D_MODEL), BF16),
            pltpu.VMEM((tm, D_MODEL), BF16),
            pltpu.VMEM((tm, ATTN_COLS), F32),
            pltpu.VMEM((tm, ATTN_COLS), F32),
            pltpu.VMEM((N_KV_HEADS, HEAD_DIM, tm + QBLK), BF16),
            pltpu.VMEM((N_KV_HEADS, tm + QBLK, LANES), BF16),
        ],
        compiler_params=pltpu.CompilerParams(
            dimension_semantics=("arbitrary",),
            vmem_limit_bytes=58 * 1024 * 1024),
        name="front",
    )(sinks, x2, gain, w_in, *tabs, qg, kg)


def _out_kernel(mixa_ref, zc_ref, x_ref, p_ref, cw_ref, wo_ref, g2_ref, wg_ref, bg_ref, wp_ref, g3_ref,
                o_ref, u_scr, *, blocks_per_seq):
    tm = o_ref.shape[0]

    @pl.when(pl.program_id(0) % blocks_per_seq == 0)
    def _():
        u_scr[0:8, :] = jnp.zeros((8, CONV_WIDTH), F32)

    halves = [slice(r * (tm // 2), (r + 1) * (tm // 2)) for r in range(2)]
    cw = cw_ref[...]
    pes = [jnp.dot(p_ref[rows, :].astype(BF16), wp_ref[...], preferred_element_type=F32) for rows in halves]
    for rows in halves:
        o_ref[rows, :] = x_ref[rows, :] + jnp.dot(mixa_ref[rows, :], wo_ref[0:ATTN_WIDTH, :],
                                                   preferred_element_type=F32)
    for rows in halves:
        y_conv = _gated_conv(zc_ref, cw, u_scr, rows, tm // 2).astype(BF16)
        o_ref[rows, :] += jnp.dot(y_conv, wo_ref[ATTN_WIDTH:, :], preferred_element_type=F32)
    u_scr[0:8, :] = u_scr[tm:tm + 8, :]
    for rows, pe in zip(halves, pes):
        x1 = o_ref[rows, :]
        ms = jnp.mean(x1 * x1, axis=-1, keepdims=True)
        hn = (x1 * lax.rsqrt(ms + EPS) * g2_ref[...]).astype(BF16)
        gate = jax.nn.sigmoid(jnp.dot(hn, wg_ref[...], preferred_element_type=F32) + bg_ref[...])
        e = pe * lax.rsqrt(jnp.mean(pe * pe, axis=-1, keepdims=True) + EPS) * g3_ref[...]
        o_ref[rows, :] = x1 + gate * e


def _out_stage(mixa, zc, x2, p2, conv_w, w_out, g2, w_gate, b_gate, w_proj, g3, seq, tm):
    t = x2.shape[0]
    row = lambda w: pl.BlockSpec((tm, w), lambda i: (i, 0))
    const = lambda r, c: pl.BlockSpec((r, c), lambda i: (0, 0), pipeline_mode=pl.Buffered(1))
    return pl.pallas_call(
        functools.partial(_out_kernel, blocks_per_seq=seq // tm),
        grid=(t // tm,),
        in_specs=[
            row(ATTN_WIDTH), row(IN_WIDTH - ATTN_COLS), row(D_MODEL), row(PLE_DIM),
            const(CONV_K, CONV_WIDTH),
            const(D_MODEL, D_MODEL), const(1, D_MODEL),
            const(D_MODEL, D_MODEL), const(1, D_MODEL),
            const(PLE_DIM, D_MODEL), const(1, D_MODEL),
        ],
        out_specs=row(D_MODEL),
        out_shape=jax.ShapeDtypeStruct((t, D_MODEL), F32),
        scratch_shapes=[pltpu.VMEM((tm + 8, CONV_WIDTH), F32)],
        compiler_params=pltpu.CompilerParams(
            dimension_semantics=("arbitrary",),
            vmem_limit_bytes=62 * 1024 * 1024),
        name="outproj_ple",
    )(mixa, zc, x2, p2, conv_w, w_out, g2, w_gate, b_gate, w_proj, g3)


def _rope_tables(s):
    half = ROT_DIM // 2
    inv_freq = jnp.power(jnp.float32(ROPE_THETA), -jnp.arange(half, dtype=F32) * 2.0 / ROT_DIM)
    ang = jnp.arange(s).astype(F32)[:, None] * inv_freq[None, :]
    cos, sin = jnp.cos(ang), jnp.sin(ang)
    pad = lambda n: jnp.zeros((s, n), F32)
    c64 = jnp.concatenate([cos, cos, jnp.ones((s, HEAD_DIM - ROT_DIM), F32)], axis=1)
    s1 = jnp.concatenate([-sin, pad(HEAD_DIM - half)], axis=1)
    s2 = jnp.concatenate([pad(half), sin, pad(HEAD_DIM - ROT_DIM)], axis=1)
    return tuple(jnp.tile(t, (1, LANES // HEAD_DIM)) for t in (c64, s1, s2))


def _layer(x, p_i, norm_gain, w_in, q_gain, k_gain, sinks, conv_w, w_out,
           gate_norm_gain, w_gate, b_gate, w_proj, ple_norm_gain):
    b, s, _ = x.shape
    t = b * s
    x2 = x.reshape(t, D_MODEL)
    qg = jnp.tile(q_gain * (HEAD_DIM ** -0.5), LANES // HEAD_DIM).reshape(1, LANES)
    kg = jnp.tile(k_gain, LANES // HEAD_DIM).reshape(1, LANES)
    mixa, zc = _front(x2, norm_gain.reshape(1, D_MODEL), w_in.astype(BF16), sinks, _rope_tables(s), qg, kg,
                      s, min(FRONT_TM, s))

    out = _out_stage(mixa, zc, x2, p_i.reshape(t, PLE_DIM), conv_w, w_out.astype(BF16),
                     gate_norm_gain.reshape(1, D_MODEL), w_gate.astype(BF16),
                     b_gate.reshape(1, D_MODEL), w_proj.astype(BF16),
                     ple_norm_gain.reshape(1, D_MODEL), s, min(OUT_TM, s))
    return out.reshape(b, s, D_MODEL)


def kernel(x, p, norm_gain, w_in, q_norm_gain, k_norm_gain, attn_sinks, conv_w, w_out,
           ple_gate_norm_gain, w_ple_gate, b_ple_gate, w_ple_proj, ple_norm_gain):
    for i in range(p.shape[0]):
        x = _layer(x, p[i], norm_gain[i], w_in[i], q_norm_gain[i], k_norm_gain[i], attn_sinks[i],
                   conv_w[i], w_out[i], ple_gate_norm_gain[i], w_ple_gate[i], b_ple_gate[i],
                   w_ple_proj[i], ple_norm_gain[i])
    return x
```

```python
import functools
import itertools

import jax
import jax.numpy as jnp
from jax import lax
from jax.experimental import pallas as pl
from jax.experimental.pallas import tpu as pltpu

D_MODEL = 2048
PLE_DIM = 256
ATTN_WIDTH = 1024
CONV_WIDTH = 1024
HEAD_DIM = 64
N_Q_HEADS = 16
N_KV_HEADS = 4
KV_WIDTH = 256
WINDOW = 128
ROT_DIM = 16
ROPE_THETA = 500000.0
CONV_K = 3
EPS = 1e-6
NEG_INF = -1e30
IN_WIDTH = 2 * ATTN_WIDTH + 2 * KV_WIDTH + 4 * CONV_WIDTH

LANES = 128
QBLK = 128

COL_Q, COL_K, COL_V, COL_GA, COL_B, COL_C, COL_H, COL_GC = 0, 1024, 1280, 1536, 2560, 3584, 4608, 5632
ATTN_COLS = COL_B
ZC_B, ZC_C, ZC_H, ZC_GC = (c - COL_B for c in (COL_B, COL_C, COL_H, COL_GC))
PROJ_CHUNK = 512
FRONT_TM = 256
FILL_PIECES = 8
TAIL_CHUNKS = 3
OUT_TM = 512

F32 = jnp.float32
BF16 = jnp.bfloat16
LOG2E = 1.4426950408889634


def _silu(g):
    half = 0.5 * g
    return half + half * jnp.tanh(half)


class _Attention:
    def __init__(self, sinks_ref, z_ref, tabs, qg, kg, mix_ref, kband, vband, first_of_seq, tm):
        self.sinks_ref, self.z_ref, self.tabs, self.qg, self.kg = sinks_ref, z_ref, tabs, qg, kg
        self.mix_ref, self.kband, self.vband, self.tm = mix_ref, kband, vband, tm
        self.units = [(qb, g) for qb in range(tm // QBLK) for g in range(N_KV_HEADS)]
        self.lo = lax.broadcasted_iota(jnp.int32, (1, LANES), 1) < HEAD_DIM
        rr = (lax.broadcasted_iota(jnp.int32, (2 * LANES, LANES), 0) % LANES) // HEAD_DIM
        cc = lax.broadcasted_iota(jnp.int32, (2 * LANES, LANES), 1) // HEAD_DIM
        self.bd = jnp.where(rr == cc, 1.0, 0.0).astype(BF16)
        qi = lax.broadcasted_iota(jnp.int32, (QBLK, 2 * QBLK), 0)
        ci = lax.broadcasted_iota(jnp.int32, (QBLK, 2 * QBLK), 1)
        band = (ci > qi) & (ci <= qi + QBLK)
        self.masks = [band & ((ci >= QBLK) | jnp.logical_not(first_of_seq))] + [band] * (tm // QBLK - 1)
        self.state = {}

    def _zcol(self, col, qb):
        return self.z_ref[qb * QBLK:(qb + 1) * QBLK, col:col + LANES]

    def _head_sumsq(self, t):
        sq = t * t
        hi = sq.astype(BF16)
        rest = (sq - hi.astype(F32)).astype(BF16)
        return jnp.dot(jnp.concatenate([hi, rest], axis=1), self.bd, preferred_element_type=F32)

    def _head_norm_rope(self, t, ss, gain, qb):
        rc, rs1, rs2 = (self.tabs[qb * QBLK:(qb + 1) * QBLK, j * LANES:(j + 1) * LANES] for j in range(3))
        y = t * lax.rsqrt(ss * (1.0 / HEAD_DIM) + EPS) * gain
        return y * rc + pltpu.roll(y, LANES - ROT_DIM // 2, 1) * rs1 + pltpu.roll(y, ROT_DIM // 2, 1) * rs2

    def _both_halves(self, t):
        tr = pltpu.roll(t, HEAD_DIM, 1)
        return jnp.where(self.lo, t, tr), jnp.where(self.lo, tr, t)

    def sumsq(self, u):
        qb, g = self.units[u]
        st = self.state[u] = {}
        if g == 0:
            st["k_raw"] = [self._zcol(COL_K + j * LANES, qb) for j in range(KV_WIDTH // LANES)]
            st["k_ss"] = [self._head_sumsq(t) for t in st["k_raw"]]
        st["q_raw"] = [self._zcol(COL_Q + (2 * g + c) * LANES, qb) for c in range(2)]
        st["q_ss"] = [self._head_sumsq(t) for t in st["q_raw"]]

    def scores(self, u):
        qb, g = self.units[u]
        st = self.state[u]
        rows = slice(QBLK + qb * QBLK, QBLK + (qb + 1) * QBLK)
        if g == 0:
            for j in range(KV_WIDTH // LANES):
                kt = self._head_norm_rope(st["k_raw"][j], st["k_ss"][j], self.kg, qb).T
                for half in range(2):
                    self.kband[2 * j + half, :, rows] = kt[half * HEAD_DIM:(half + 1) * HEAD_DIM].astype(BF16)
                for half, t in enumerate(self._both_halves(self._zcol(COL_V + j * LANES, qb))):
                    self.vband[2 * j + half, rows, :] = t.astype(BF16)
        q = jnp.concatenate(
            [self._head_norm_rope(t, ss, self.qg, qb).astype(BF16) for t, ss in zip(st["q_raw"], st["q_ss"])],
            axis=0)
        kt = self.kband[g, :, qb * QBLK:(qb + 2) * QBLK]
        zero = jnp.zeros_like(kt)
        st["s"] = [jnp.dot(q, jnp.concatenate([kt, zero], axis=0), preferred_element_type=F32),
                   jnp.dot(q, jnp.concatenate([zero, kt], axis=0), preferred_element_type=F32)]

    def output(self, u):
        qb, g = self.units[u]
        st = self.state.pop(u)
        vt = self.vband[g, qb * QBLK:(qb + 2) * QBLK, :]
        zero = jnp.zeros_like(vt)
        v2 = jnp.concatenate([jnp.where(self.lo, vt, zero), jnp.where(self.lo, zero, vt)], axis=0)
        for c in range(2):
            p = 2 * g + c
            es, inv = [], []
            for half in range(2):
                sink = self.sinks_ref[2 * p + half]
                sc = jnp.where(self.masks[qb], st["s"][half][c * QBLK:(c + 1) * QBLK], NEG_INF)
                m = jnp.maximum(jnp.max(sc, axis=-1, keepdims=True), sink)
                e = jnp.exp2(sc - m)
                den = jnp.sum(e, axis=-1, keepdims=True) + jnp.exp2(sink - m)
                es.append(e.astype(BF16))
                inv.append(1.0 / den)
            o = jnp.dot(jnp.concatenate(es, axis=1), v2, preferred_element_type=F32)
            o = o * jnp.where(self.lo, inv[0], inv[1])
            gate = self._zcol(COL_GA + p * LANES, qb)
            self.mix_ref[qb * QBLK:(qb + 1) * QBLK, p * LANES:(p + 1) * LANES] = (
                o * _silu(gate)).astype(BF16)

    def carry(self):
        self.kband[:, :, 0:QBLK] = self.kband[:, :, self.tm:self.tm + QBLK]
        self.vband[:, 0:QBLK, :] = self.vband[:, self.tm:self.tm + QBLK, :]


def _gated_conv(zc_ref, cw, u_scr, rows, n_rows):
    seg = lambda col: zc_ref[rows, col:col + CONV_WIDTH]
    u = seg(ZC_C) * seg(ZC_H)
    r0 = 8 + rows.start
    u_scr[r0:r0 + n_rows, :] = u
    conv = cw[0:1] * u_scr[r0 - 2:r0 - 2 + n_rows, :] + cw[1:2] * u_scr[r0 - 1:r0 - 1 + n_rows, :] + cw[2:3] * u
    gc = seg(ZC_GC)
    return seg(ZC_B) * conv * _silu(gc)


def _front_kernel(sinks_ref, x_ref, g_ref, w_ref, rope_ref, qg_ref, kg_ref,
                  mix_ref, zc_ref, h0, h1, z0, z1, kband, vband, *, tm, blocks_per_seq):
    i = pl.program_id(0)
    first_of_seq = (i + 2 * blocks_per_seq - 2) % blocks_per_seq == 0

    @pl.when(i == 0)
    def _():
        h1[...] = jnp.zeros_like(h1)
        z0[...] = jnp.zeros_like(z0)
        kband[...] = jnp.zeros_like(kband)
        vband[...] = jnp.zeros_like(vband)

    @pl.when(first_of_seq)
    def _():
        kband[:, :, 0:QBLK] = jnp.zeros((N_KV_HEADS, HEAD_DIM, QBLK), BF16)
        vband[:, 0:QBLK, :] = jnp.zeros((N_KV_HEADS, QBLK, LANES), BF16)

    def body(h_w, h_r, z_w, z_r):
        def norm_piece(r0, r1):
            x = x_ref[r0:r1, :]
            ms = jnp.mean(x * x, axis=-1, keepdims=True)
            h_w[r0:r1, :] = (x * lax.rsqrt(ms + EPS) * g_ref[...]).astype(BF16)

        rw_ = tm // FILL_PIECES
        fillers = iter([functools.partial(norm_piece, j * rw_, (j + 1) * rw_) for j in range(FILL_PIECES)])
        chunks = iter(range(IN_WIDTH // PROJ_CHUNK))

        def project(count):
            for n in itertools.islice(chunks, count):
                res = jnp.dot(h_r[...], w_ref[:, n * PROJ_CHUNK:(n + 1) * PROJ_CHUNK], preferred_element_type=F32)
                if n < ATTN_COLS // PROJ_CHUNK:
                    z_w[:, n * PROJ_CHUNK:(n + 1) * PROJ_CHUNK] = res
                else:
                    zc_ref[:, n * PROJ_CHUNK - ATTN_COLS:(n + 1) * PROJ_CHUNK - ATTN_COLS] = res

        def fill(count):
            for f in itertools.islice(fillers, count):
                f()

        att = _Attention(sinks_ref, z_r, rope_ref, qg_ref[...], kg_ref[...],
                         mix_ref, kband, vband, first_of_seq, tm)
        n_units = len(att.units)
        att.sumsq(0)
        project(1)
        for u in range(n_units):
            if u > 0:
                att.output(u - 1)
            att.scores(u)
            if u + 1 < n_units:
                att.sumsq(u + 1)
            project(1)
        att.output(n_units - 1)
        att.carry()
        project(1)
        for _ in range(TAIL_CHUNKS):
            project(1)
        fill(FILL_PIECES)

    @pl.when(i % 2 == 0)
    def _():
        body(h0, h1, z1, z0)

    @pl.when(i % 2 == 1)
    def _():
        body(h1, h0, z0, z1)


def _front(x2, gain, w_in, sinks, tabs, qg, kg, seq, tm):
    t = x2.shape[0]
    nblk = t // tm
    bps = seq // tm
    const = lambda r, c: pl.BlockSpec((r, c), lambda i: (0, 0), pipeline_mode=pl.Buffered(1))
    tab = pl.BlockSpec((tm, 3 * LANES), lambda i: ((i + 2 * bps - 2) % bps, 0))
    return pl.pallas_call(
        functools.partial(_front_kernel, tm=tm, blocks_per_seq=bps),
        grid=(nblk + 2,),
        in_specs=[
            pl.BlockSpec(memory_space=pltpu.SMEM),
            pl.BlockSpec((tm, D_MODEL), lambda i: (jnp.minimum(i, nblk - 1), 0)),
            const(1, D_MODEL),
            const(D_MODEL, IN_WIDTH),
            tab,
            const(1, LANES), const(1, LANES),
        ],
        out_specs=[
            pl.BlockSpec((tm, ATTN_WIDTH), lambda i: (jnp.maximum(i - 2, 0), 0)),
            pl.BlockSpec((tm, IN_WIDTH - ATTN_COLS), lambda i: (jnp.clip(i - 1, 0, nblk - 1), 0)),
        ],
        out_shape=[
            jax.ShapeDtypeStruct((t, ATTN_WIDTH), BF16),
            jax.ShapeDtypeStruct((t, IN_WIDTH - ATTN_COLS), F32),
        ],
        scratch_shapes=[
            pltpu.VMEM((tm, D_MODEL), BF16),
            pltpu.VMEM((tm, D_MODEL), BF16),
            pltpu.VMEM((tm, ATTN_COLS), F32),
            pltpu.VMEM((tm, ATTN_COLS), F32),
            pltpu.VMEM((N_KV_HEADS, HEAD_DIM, tm + QBLK), BF16),
            pltpu.VMEM((N_KV_HEADS, tm + QBLK, LANES), BF16),
        ],
        compiler_params=pltpu.CompilerParams(
            dimension_semantics=("arbitrary",),
            vmem_limit_bytes=58 * 1024 * 1024),
        name="front",
    )(sinks, x2, gain, w_in, tabs, qg, kg)


def _out_kernel(mixa_ref, zc_ref, x_ref, p_ref, cw_ref, wo_ref, g2_ref, wg_ref, bg_ref, wp_ref, g3_ref,
                o_ref, u_scr, *, blocks_per_seq):
    tm = o_ref.shape[0]

    @pl.when(pl.program_id(0) % blocks_per_seq == 0)
    def _():
        u_scr[0:8, :] = jnp.zeros((8, CONV_WIDTH), F32)

    halves = [slice(r * (tm // 2), (r + 1) * (tm // 2)) for r in range(2)]
    cw = cw_ref[...]
    pes = [jnp.dot(p_ref[rows, :].astype(BF16), wp_ref[...], preferred_element_type=F32) for rows in halves]
    for rows in halves:
        o_ref[rows, :] = x_ref[rows, :] + jnp.dot(mixa_ref[rows, :], wo_ref[0:ATTN_WIDTH, :],
                                                   preferred_element_type=F32)
    for rows in halves:
        y_conv = _gated_conv(zc_ref, cw, u_scr, rows, tm // 2).astype(BF16)
        o_ref[rows, :] += jnp.dot(y_conv, wo_ref[ATTN_WIDTH:, :], preferred_element_type=F32)
    u_scr[0:8, :] = u_scr[tm:tm + 8, :]
    for rows, pe in zip(halves, pes):
        x1 = o_ref[rows, :]
        ms = jnp.mean(x1 * x1, axis=-1, keepdims=True)
        hn = (x1 * lax.rsqrt(ms + EPS) * g2_ref[...]).astype(BF16)
        t = jnp.tanh(0.5 * (jnp.dot(hn, wg_ref[...], preferred_element_type=F32) + bg_ref[...]))
        e_half = pe * lax.rsqrt(jnp.mean(pe * pe, axis=-1, keepdims=True) + EPS) * (0.5 * g3_ref[...])
        o_ref[rows, :] = (x1 + e_half) + e_half * t


def _out_stage(mixa, zc, x2, p2, conv_w, w_out, g2, w_gate, b_gate, w_proj, g3, seq, tm):
    t = x2.shape[0]
    row = lambda w: pl.BlockSpec((tm, w), lambda i: (i, 0))
    const = lambda r, c: pl.BlockSpec((r, c), lambda i: (0, 0), pipeline_mode=pl.Buffered(1))
    return pl.pallas_call(
        functools.partial(_out_kernel, blocks_per_seq=seq // tm),
        grid=(t // tm,),
        in_specs=[
            row(ATTN_WIDTH), row(IN_WIDTH - ATTN_COLS), row(D_MODEL), row(PLE_DIM),
            const(CONV_K, CONV_WIDTH),
            const(D_MODEL, D_MODEL), const(1, D_MODEL),
            const(D_MODEL, D_MODEL), const(1, D_MODEL),
            const(PLE_DIM, D_MODEL), const(1, D_MODEL),
        ],
        out_specs=row(D_MODEL),
        out_shape=jax.ShapeDtypeStruct((t, D_MODEL), F32),
        scratch_shapes=[pltpu.VMEM((tm + 8, CONV_WIDTH), F32)],
        compiler_params=pltpu.CompilerParams(
            dimension_semantics=("arbitrary",),
            vmem_limit_bytes=62 * 1024 * 1024),
        name="outproj_ple",
    )(mixa, zc, x2, p2, conv_w, w_out, g2, w_gate, b_gate, w_proj, g3)


def _rope_tables(s):
    half = ROT_DIM // 2
    inv_freq = jnp.power(jnp.float32(ROPE_THETA), -jnp.arange(half, dtype=F32) * 2.0 / ROT_DIM)
    ang = jnp.arange(s).astype(F32)[:, None] * inv_freq[None, :]
    cos, sin = jnp.cos(ang), jnp.sin(ang)
    pad = lambda n: jnp.zeros((s, n), F32)
    c64 = jnp.concatenate([cos, cos, jnp.ones((s, HEAD_DIM - ROT_DIM), F32)], axis=1)
    s1 = jnp.concatenate([-sin, pad(HEAD_DIM - half)], axis=1)
    s2 = jnp.concatenate([pad(half), sin, pad(HEAD_DIM - ROT_DIM)], axis=1)
    return jnp.concatenate([jnp.tile(t, (1, LANES // HEAD_DIM)) for t in (c64, s1, s2)], axis=1)


def _layer(x, p_i, norm_gain, w_in, q_gain, k_gain, sinks, conv_w, w_out,
           gate_norm_gain, w_gate, b_gate, w_proj, ple_norm_gain):
    b, s, _ = x.shape
    t = b * s
    x2 = x.reshape(t, D_MODEL)
    qg = jnp.tile(q_gain * (HEAD_DIM ** -0.5 * LOG2E), LANES // HEAD_DIM).reshape(1, LANES)
    kg = jnp.tile(k_gain, LANES // HEAD_DIM).reshape(1, LANES)
    mixa, zc = _front(x2, norm_gain.reshape(1, D_MODEL), w_in.astype(BF16), sinks * LOG2E, _rope_tables(s), qg, kg,
                      s, min(FRONT_TM, s))

    out = _out_stage(mixa, zc, x2, p_i.reshape(t, PLE_DIM), conv_w, w_out.astype(BF16),
                     gate_norm_gain.reshape(1, D_MODEL), w_gate.astype(BF16),
                     b_gate.reshape(1, D_MODEL), w_proj.astype(BF16),
                     ple_norm_gain.reshape(1, D_MODEL), s, min(OUT_TM, s))
    return out.reshape(b, s, D_MODEL)


def kernel(x, p, norm_gain, w_in, q_norm_gain, k_norm_gain, attn_sinks, conv_w, w_out,
           ple_gate_norm_gain, w_ple_gate, b_ple_gate, w_ple_proj, ple_norm_gain):
    for i in range(p.shape[0]):
        x = _layer(x, p[i], norm_gain[i], w_in[i], q_norm_gain[i], k_norm_gain[i], attn_sinks[i],
                   conv_w[i], w_out[i], ple_gate_norm_gain[i], w_ple_gate[i], b_ple_gate[i],
                   w_ple_proj[i], ple_norm_gain[i])
    return x
```

```python
import functools
import itertools

import jax
import jax.numpy as jnp
from jax import lax
from jax.experimental import pallas as pl
from jax.experimental.pallas import tpu as pltpu

D_MODEL = 2048
PLE_DIM = 256
ATTN_WIDTH = 1024
CONV_WIDTH = 1024
HEAD_DIM = 64
N_Q_HEADS = 16
N_KV_HEADS = 4
KV_WIDTH = 256
WINDOW = 128
ROT_DIM = 16
ROPE_THETA = 500000.0
CONV_K = 3
EPS = 1e-6
NEG_INF = -1e30
IN_WIDTH = 2 * ATTN_WIDTH + 2 * KV_WIDTH + 4 * CONV_WIDTH

LANES = 128
QBLK = 128

COL_Q, COL_K, COL_V, COL_GA, COL_B, COL_C, COL_H, COL_GC = 0, 1024, 1280, 1536, 2560, 3584, 4608, 5632
ATTN_COLS = COL_B
ZC_B, ZC_C, ZC_H, ZC_GC = (c - COL_B for c in (COL_B, COL_C, COL_H, COL_GC))
PROJ_CHUNK = 512
FRONT_TM = 256
FILL_PIECES = 8
TAIL_CHUNKS = 3
OUT_TM = 512

F32 = jnp.float32
BF16 = jnp.bfloat16
LOG2E = 1.4426950408889634


def _silu(g):
    half = 0.5 * g
    return half + half * jnp.tanh(half)


class _Attention:
    def __init__(self, sinks_ref, z_ref, tabs, qg, kg, mix_ref, kband, vband, first_of_seq, tm):
        self.sinks_ref, self.z_ref, self.tabs, self.qg, self.kg = sinks_ref, z_ref, tabs, qg, kg
        self.mix_ref, self.kband, self.vband, self.tm = mix_ref, kband, vband, tm
        self.units = [(qb, g) for qb in range(tm // QBLK) for g in range(N_KV_HEADS)]
        self.lo = lax.broadcasted_iota(jnp.int32, (1, LANES), 1) < HEAD_DIM
        rr = lax.broadcasted_iota(jnp.int32, (LANES, LANES), 0) // HEAD_DIM
        cc = lax.broadcasted_iota(jnp.int32, (LANES, LANES), 1) // HEAD_DIM
        self.bd = jnp.where(rr == cc, 1.0, 0.0).astype(BF16)
        qi = lax.broadcasted_iota(jnp.int32, (QBLK, 2 * QBLK), 0)
        ci = lax.broadcasted_iota(jnp.int32, (QBLK, 2 * QBLK), 1)
        band = (ci > qi) & (ci <= qi + QBLK)
        self.masks = [band & ((ci >= QBLK) | jnp.logical_not(first_of_seq))] + [band] * (tm // QBLK - 1)
        self.state = {}

    def _zcol(self, col, qb):
        return self.z_ref[qb * QBLK:(qb + 1) * QBLK, col:col + LANES]

    def _head_sumsq(self, t):
        return jnp.dot((t * t).astype(BF16), self.bd, preferred_element_type=F32)

    def _head_norm_rope(self, t, ss, gain, qb):
        rc, rs1, rs2 = (self.tabs[qb * QBLK:(qb + 1) * QBLK, j * LANES:(j + 1) * LANES] for j in range(3))
        y = t * lax.rsqrt(ss * (1.0 / HEAD_DIM) + EPS) * gain
        return y * rc + pltpu.roll(y, LANES - ROT_DIM // 2, 1) * rs1 + pltpu.roll(y, ROT_DIM // 2, 1) * rs2

    def _both_halves(self, t):
        tr = pltpu.roll(t, HEAD_DIM, 1)
        return jnp.where(self.lo, t, tr), jnp.where(self.lo, tr, t)

    def sumsq(self, u):
        qb, g = self.units[u]
        st = self.state[u] = {}
        if g == 0:
            st["k_raw"] = [self._zcol(COL_K + j * LANES, qb) for j in range(KV_WIDTH // LANES)]
            st["k_ss"] = [self._head_sumsq(t) for t in st["k_raw"]]
        st["q_raw"] = [self._zcol(COL_Q + (2 * g + c) * LANES, qb) for c in range(2)]
        st["q_ss"] = [self._head_sumsq(t) for t in st["q_raw"]]

    def scores(self, u):
        qb, g = self.units[u]
        st = self.state[u]
        rows = slice(QBLK + qb * QBLK, QBLK + (qb + 1) * QBLK)
        if g == 0:
            for j in range(KV_WIDTH // LANES):
                kt = self._head_norm_rope(st["k_raw"][j], st["k_ss"][j], self.kg, qb).T
                for half in range(2):
                    self.kband[2 * j + half, :, rows] = kt[half * HEAD_DIM:(half + 1) * HEAD_DIM].astype(BF16)
                for half, t in enumerate(self._both_halves(self._zcol(COL_V + j * LANES, qb))):
                    self.vband[2 * j + half, rows, :] = t.astype(BF16)
        q = jnp.concatenate(
            [self._head_norm_rope(t, ss, self.qg, qb).astype(BF16) for t, ss in zip(st["q_raw"], st["q_ss"])],
            axis=0)
        kt = self.kband[g, :, qb * QBLK:(qb + 2) * QBLK]
        zero = jnp.zeros_like(kt)
        st["s"] = [jnp.dot(q, jnp.concatenate([kt, zero], axis=0), preferred_element_type=F32),
                   jnp.dot(q, jnp.concatenate([zero, kt], axis=0), preferred_element_type=F32)]

    def output(self, u):
        qb, g = self.units[u]
        st = self.state.pop(u)
        vt = self.vband[g, qb * QBLK:(qb + 2) * QBLK, :]
        zero = jnp.zeros_like(vt)
        v2 = jnp.concatenate([jnp.where(self.lo, vt, zero), jnp.where(self.lo, zero, vt)], axis=0)
        for c in range(2):
            p = 2 * g + c
            es, inv = [], []
            for half in range(2):
                sink = self.sinks_ref[2 * p + half]
                sc = jnp.where(self.masks[qb], st["s"][half][c * QBLK:(c + 1) * QBLK], NEG_INF)
                m = jnp.maximum(jnp.max(sc, axis=-1, keepdims=True), sink)
                e = jnp.exp2(sc - m)
                den = jnp.sum(e, axis=-1, keepdims=True) + jnp.exp2(sink - m)
                es.append(e.astype(BF16))
                inv.append(1.0 / den)
            o = jnp.dot(jnp.concatenate(es, axis=1), v2, preferred_element_type=F32)
            o = o * jnp.where(self.lo, inv[0], inv[1])
            gate = self._zcol(COL_GA + p * LANES, qb)
            self.mix_ref[qb * QBLK:(qb + 1) * QBLK, p * LANES:(p + 1) * LANES] = (
                o * _silu(gate)).astype(BF16)

    def carry(self):
        self.kband[:, :, 0:QBLK] = self.kband[:, :, self.tm:self.tm + QBLK]
        self.vband[:, 0:QBLK, :] = self.vband[:, self.tm:self.tm + QBLK, :]


def _gated_conv(zc_ref, cw, u_scr, rows, n_rows):
    seg = lambda col: zc_ref[rows, col:col + CONV_WIDTH]
    u = seg(ZC_C) * seg(ZC_H)
    r0 = 8 + rows.start
    u_scr[r0:r0 + n_rows, :] = u
    conv = cw[0:1] * u_scr[r0 - 2:r0 - 2 + n_rows, :] + cw[1:2] * u_scr[r0 - 1:r0 - 1 + n_rows, :] + cw[2:3] * u
    gc = seg(ZC_GC)
    return seg(ZC_B) * conv * _silu(gc)


def _front_kernel(sinks_ref, x_ref, g_ref, w_ref, rope_ref, qg_ref, kg_ref,
                  mix_ref, zc_ref, h0, h1, z0, z1, kband, vband, *, tm, blocks_per_seq):
    i = pl.program_id(0)
    first_of_seq = (i + 2 * blocks_per_seq - 2) % blocks_per_seq == 0

    @pl.when(i == 0)
    def _():
        h1[...] = jnp.zeros_like(h1)
        z0[...] = jnp.zeros_like(z0)
        kband[...] = jnp.zeros_like(kband)
        vband[...] = jnp.zeros_like(vband)

    @pl.when(first_of_seq)
    def _():
        kband[:, :, 0:QBLK] = jnp.zeros((N_KV_HEADS, HEAD_DIM, QBLK), BF16)
        vband[:, 0:QBLK, :] = jnp.zeros((N_KV_HEADS, QBLK, LANES), BF16)

    def body(h_w, h_r, z_w, z_r):
        def norm_piece(r0, r1):
            x = x_ref[r0:r1, :]
            ms = jnp.mean(x * x, axis=-1, keepdims=True)
            h_w[r0:r1, :] = (x * lax.rsqrt(ms + EPS) * g_ref[...]).astype(BF16)

        rw_ = tm // FILL_PIECES
        fillers = iter([functools.partial(norm_piece, j * rw_, (j + 1) * rw_) for j in range(FILL_PIECES)])
        chunks = iter(range(IN_WIDTH // PROJ_CHUNK))

        def project(count):
            for n in itertools.islice(chunks, count):
                res = jnp.dot(h_r[...], w_ref[:, n * PROJ_CHUNK:(n + 1) * PROJ_CHUNK], preferred_element_type=F32)
                if n < ATTN_COLS // PROJ_CHUNK:
                    z_w[:, n * PROJ_CHUNK:(n + 1) * PROJ_CHUNK] = res
                else:
                    zc_ref[:, n * PROJ_CHUNK - ATTN_COLS:(n + 1) * PROJ_CHUNK - ATTN_COLS] = res

        def fill(count):
            for f in itertools.islice(fillers, count):
                f()

        att = _Attention(sinks_ref, z_r, rope_ref, qg_ref[...], kg_ref[...],
                         mix_ref, kband, vband, first_of_seq, tm)
        n_units = len(att.units)
        att.sumsq(0)
        project(1)
        for u in range(n_units):
            if u > 0:
                att.output(u - 1)
            att.scores(u)
            if u + 1 < n_units:
                att.sumsq(u + 1)
            project(1)
        att.output(n_units - 1)
        att.carry()
        project(1)
        for _ in range(TAIL_CHUNKS):
            project(1)
        fill(FILL_PIECES)

    @pl.when(i % 2 == 0)
    def _():
        body(h0, h1, z1, z0)

    @pl.when(i % 2 == 1)
    def _():
        body(h1, h0, z0, z1)


def _front(x2, gain, w_in, sinks, tabs, qg, kg, seq, tm):
    t = x2.shape[0]
    nblk = t // tm
    bps = seq // tm
    const = lambda r, c: pl.BlockSpec((r, c), lambda i: (0, 0), pipeline_mode=pl.Buffered(1))
    tab = pl.BlockSpec((tm, 3 * LANES), lambda i: ((i + 2 * bps - 2) % bps, 0))
    return pl.pallas_call(
        functools.partial(_front_kernel, tm=tm, blocks_per_seq=bps),
        grid=(nblk + 2,),
        in_specs=[
            pl.BlockSpec(memory_space=pltpu.SMEM),
            pl.BlockSpec((tm, D_MODEL), lambda i: (jnp.minimum(i, nblk - 1), 0)),
            const(1, D_MODEL),
            const(D_MODEL, IN_WIDTH),
            tab,
            const(1, LANES), const(1, LANES),
        ],
        out_specs=[
            pl.BlockSpec((tm, ATTN_WIDTH), lambda i: (jnp.maximum(i - 2, 0), 0)),
            pl.BlockSpec((tm, IN_WIDTH - ATTN_COLS), lambda i: (jnp.clip(i - 1, 0, nblk - 1), 0)),
        ],
        out_shape=[
            jax.ShapeDtypeStruct((t, ATTN_WIDTH), BF16),
            jax.ShapeDtypeStruct((t, IN_WIDTH - ATTN_COLS), F32),
        ],
        scratch_shapes=[
            pltpu.VMEM((tm, D_MODEL), BF16),
            pltpu.VMEM((tm, D_MODEL), BF16),
            pltpu.VMEM((tm, ATTN_COLS), F32),
            pltpu.VMEM((tm, ATTN_COLS), F32),
            pltpu.VMEM((N_KV_HEADS, HEAD_DIM, tm + QBLK), BF16),
            pltpu.VMEM((N_KV_HEADS, tm + QBLK, LANES), BF16),
        ],
        compiler_params=pltpu.CompilerParams(
            dimension_semantics=("arbitrary",),
            vmem_limit_bytes=58 * 1024 * 1024),
        name="front",
    )(sinks, x2, gain, w_in, tabs, qg, kg)


def _out_kernel(mixa_ref, zc_ref, x_ref, p_ref, cw_ref, wo_ref, g2_ref, wg_ref, bg_ref, wp_ref, g3_ref,
                o_ref, u_scr, *, blocks_per_seq):
    tm = o_ref.shape[0]

    @pl.when(pl.program_id(0) % blocks_per_seq == 0)
    def _():
        u_scr[0:8, :] = jnp.zeros((8, CONV_WIDTH), F32)

    halves = [slice(r * (tm // 2), (r + 1) * (tm // 2)) for r in range(2)]
    cw = cw_ref[...]
    pes = [jnp.dot(p_ref[rows, :].astype(BF16), wp_ref[...], preferred_element_type=F32) for rows in halves]
    for rows in halves:
        y_conv = _gated_conv(zc_ref, cw, u_scr, rows, tm // 2).astype(BF16)
        mix = jnp.concatenate([mixa_ref[rows, :], y_conv], axis=1)
        o_ref[rows, :] = x_ref[rows, :] + jnp.dot(mix, wo_ref[...], preferred_element_type=F32)
    u_scr[0:8, :] = u_scr[tm:tm + 8, :]
    for rows, pe in zip(halves, pes):
        x1 = o_ref[rows, :]
        ms = jnp.mean(x1 * x1, axis=-1, keepdims=True)
        hn = (x1 * lax.rsqrt(ms + EPS) * g2_ref[...]).astype(BF16)
        t = jnp.tanh(0.5 * (jnp.dot(hn, wg_ref[...], preferred_element_type=F32) + bg_ref[...]))
        e_half = pe * lax.rsqrt(jnp.mean(pe * pe, axis=-1, keepdims=True) + EPS) * (0.5 * g3_ref[...])
        o_ref[rows, :] = (x1 + e_half) + e_half * t


def _out_stage(mixa, zc, x2, p2, conv_w, w_out, g2, w_gate, b_gate, w_proj, g3, seq, tm):
    t = x2.shape[0]
    row = lambda w: pl.BlockSpec((tm, w), lambda i: (i, 0))
    const = lambda r, c: pl.BlockSpec((r, c), lambda i: (0, 0), pipeline_mode=pl.Buffered(1))
    return pl.pallas_call(
        functools.partial(_out_kernel, blocks_per_seq=seq // tm),
        grid=(t // tm,),
        in_specs=[
            row(ATTN_WIDTH), row(IN_WIDTH - ATTN_COLS), row(D_MODEL), row(PLE_DIM),
            const(CONV_K, CONV_WIDTH),
            const(D_MODEL, D_MODEL), const(1, D_MODEL),
            const(D_MODEL, D_MODEL), const(1, D_MODEL),
            const(PLE_DIM, D_MODEL), const(1, D_MODEL),
        ],
        out_specs=row(D_MODEL),
        out_shape=jax.ShapeDtypeStruct((t, D_MODEL), F32),
        scratch_shapes=[pltpu.VMEM((tm + 8, CONV_WIDTH), F32)],
        compiler_params=pltpu.CompilerParams(
            dimension_semantics=("arbitrary",),
            vmem_limit_bytes=62 * 1024 * 1024),
        name="outproj_ple",
    )(mixa, zc, x2, p2, conv_w, w_out, g2, w_gate, b_gate, w_proj, g3)


def _rope_tables(s):
    half = ROT_DIM // 2
    inv_freq = jnp.power(jnp.float32(ROPE_THETA), -jnp.arange(half, dtype=F32) * 2.0 / ROT_DIM)
    ang = jnp.arange(s).astype(F32)[:, None] * inv_freq[None, :]
    cos, sin = jnp.cos(ang), jnp.sin(ang)
    pad = lambda n: jnp.zeros((s, n), F32)
    c64 = jnp.concatenate([cos, cos, jnp.ones((s, HEAD_DIM - ROT_DIM), F32)], axis=1)
    s1 = jnp.concatenate([-sin, pad(HEAD_DIM - half)], axis=1)
    s2 = jnp.concatenate([pad(half), sin, pad(HEAD_DIM - ROT_DIM)], axis=1)
    return jnp.concatenate([jnp.tile(t, (1, LANES // HEAD_DIM)) for t in (c64, s1, s2)], axis=1)


def _layer(x, p_i, norm_gain, w_in, q_gain, k_gain, sinks, conv_w, w_out,
           gate_norm_gain, w_gate, b_gate, w_proj, ple_norm_gain):
    b, s, _ = x.shape
    t = b * s
    x2 = x.reshape(t, D_MODEL)
    qg = jnp.tile(q_gain * (HEAD_DIM ** -0.5 * LOG2E), LANES // HEAD_DIM).reshape(1, LANES)
    kg = jnp.tile(k_gain, LANES // HEAD_DIM).reshape(1, LANES)
    mixa, zc = _front(x2, norm_gain.reshape(1, D_MODEL), w_in.astype(BF16), sinks * LOG2E, _rope_tables(s), qg, kg,
                      s, min(FRONT_TM, s))

    out = _out_stage(mixa, zc, x2, p_i.reshape(t, PLE_DIM), conv_w, w_out.astype(BF16),
                     gate_norm_gain.reshape(1, D_MODEL), w_gate.astype(BF16),
                     b_gate.reshape(1, D_MODEL), w_proj.astype(BF16),
                     ple_norm_gain.reshape(1, D_MODEL), s, min(OUT_TM, s))
    return out.reshape(b, s, D_MODEL)


def kernel(x, p, norm_gain, w_in, q_norm_gain, k_norm_gain, attn_sinks, conv_w, w_out,
           ple_gate_norm_gain, w_ple_gate, b_ple_gate, w_ple_proj, ple_norm_gain):
    for i in range(p.shape[0]):
        x = _layer(x, p[i], norm_gain[i], w_in[i], q_norm_gain[i], k_norm_gain[i], attn_sinks[i],
                   conv_w[i], w_out[i], ple_gate_norm_gain[i], w_ple_gate[i], b_ple_gate[i],
                   w_ple_proj[i], ple_norm_gain[i])
    return x
```

```python
import functools
import itertools

import jax
import jax.numpy as jnp
from jax import lax
from jax.experimental import pallas as pl
from jax.experimental.pallas import tpu as pltpu

D_MODEL = 2048
PLE_DIM = 256
ATTN_WIDTH = 1024
CONV_WIDTH = 1024
HEAD_DIM = 64
N_Q_HEADS = 16
N_KV_HEADS = 4
KV_WIDTH = 256
WINDOW = 128
ROT_DIM = 16
ROPE_THETA = 500000.0
CONV_K = 3
EPS = 1e-6
NEG_INF = -1e30
IN_WIDTH = 2 * ATTN_WIDTH + 2 * KV_WIDTH + 4 * CONV_WIDTH

LANES = 128
QBLK = 128

COL_Q, COL_K, COL_V, COL_GA, COL_B, COL_C, COL_H, COL_GC = 0, 1024, 1280, 1536, 2560, 3584, 4608, 5632
ATTN_COLS = COL_B
ZC_B, ZC_C, ZC_H, ZC_GC = (c - COL_B for c in (COL_B, COL_C, COL_H, COL_GC))
PROJ_CHUNK = 512
FRONT_TM = 256
FILL_PIECES = 8
TAIL_CHUNKS = 3
OUT_TM = 512

F32 = jnp.float32
BF16 = jnp.bfloat16
LOG2E = 1.4426950408889634


def _silu(g):
    half = 0.5 * g
    return half + half * jnp.tanh(half)


class _Attention:
    def __init__(self, sinks_ref, z_ref, tabs, bias_ref, qg, kg, mix_ref, kband, vband, first_of_seq, tm):
        self.sinks_ref, self.z_ref, self.tabs, self.qg, self.kg = sinks_ref, z_ref, tabs, qg, kg
        self.mix_ref, self.kband, self.vband, self.tm = mix_ref, kband, vband, tm
        self.units = [(qb, g) for qb in range(tm // QBLK) for g in range(N_KV_HEADS)]
        self.lo = lax.broadcasted_iota(jnp.int32, (1, LANES), 1) < HEAD_DIM
        rr = (lax.broadcasted_iota(jnp.int32, (2 * LANES, LANES), 0) % LANES) // HEAD_DIM
        cc = lax.broadcasted_iota(jnp.int32, (2 * LANES, LANES), 1) // HEAD_DIM
        self.bd = jnp.where(rr == cc, 1.0, 0.0).astype(BF16)
        self.bias = [bias_ref.at[first_of_seq.astype(jnp.int32)]] + [bias_ref.at[0]] * (tm // QBLK - 1)
        self.state = {}

    def _zcol(self, col, qb):
        return self.z_ref[qb * QBLK:(qb + 1) * QBLK, col:col + LANES]

    def _head_sumsq(self, t):
        sq = t * t
        hi = sq.astype(BF16)
        rest = (sq - hi.astype(F32)).astype(BF16)
        return jnp.dot(jnp.concatenate([hi, rest], axis=1), self.bd, preferred_element_type=F32)

    def _head_norm_rope(self, t, ss, gain, qb):
        rc, rs1, rs2 = (self.tabs[qb * QBLK:(qb + 1) * QBLK, j * LANES:(j + 1) * LANES] for j in range(3))
        y = t * lax.rsqrt(ss * (1.0 / HEAD_DIM) + EPS) * gain
        return y * rc + pltpu.roll(y, LANES - ROT_DIM // 2, 1) * rs1 + pltpu.roll(y, ROT_DIM // 2, 1) * rs2

    def _both_halves(self, t):
        tr = pltpu.roll(t, HEAD_DIM, 1)
        return jnp.where(self.lo, t, tr), jnp.where(self.lo, tr, t)

    def sumsq(self, u):
        qb, g = self.units[u]
        st = self.state[u] = {}
        if g == 0:
            st["k_raw"] = [self._zcol(COL_K + j * LANES, qb) for j in range(KV_WIDTH // LANES)]
            st["k_ss"] = [self._head_sumsq(t) for t in st["k_raw"]]
        st["q_raw"] = [self._zcol(COL_Q + (2 * g + c) * LANES, qb) for c in range(2)]
        st["q_ss"] = [self._head_sumsq(t) for t in st["q_raw"]]

    def scores(self, u):
        qb, g = self.units[u]
        st = self.state[u]
        rows = slice(QBLK + qb * QBLK, QBLK + (qb + 1) * QBLK)
        if g == 0:
            for j in range(KV_WIDTH // LANES):
                kt = self._head_norm_rope(st["k_raw"][j], st["k_ss"][j], self.kg, qb).T
                for half in range(2):
                    self.kband[2 * j + half, :, rows] = kt[half * HEAD_DIM:(half + 1) * HEAD_DIM].astype(BF16)
                for half, t in enumerate(self._both_halves(self._zcol(COL_V + j * LANES, qb))):
                    self.vband[2 * j + half, rows, :] = t.astype(BF16)
        q = jnp.concatenate(
            [self._head_norm_rope(t, ss, self.qg, qb).astype(BF16) for t, ss in zip(st["q_raw"], st["q_ss"])],
            axis=0)
        kt = self.kband[g, :, qb * QBLK:(qb + 2) * QBLK]
        zero = jnp.zeros_like(kt)
        st["s"] = [jnp.dot(q, jnp.concatenate([kt, zero], axis=0), preferred_element_type=F32),
                   jnp.dot(q, jnp.concatenate([zero, kt], axis=0), preferred_element_type=F32)]

    def output(self, u):
        qb, g = self.units[u]
        st = self.state.pop(u)
        vt = self.vband[g, qb * QBLK:(qb + 2) * QBLK, :]
        zero = jnp.zeros_like(vt)
        v2 = jnp.concatenate([jnp.where(self.lo, vt, zero), jnp.where(self.lo, zero, vt)], axis=0)
        for c in range(2):
            p = 2 * g + c
            es, inv = [], []
            for half in range(2):
                sink = self.sinks_ref[2 * p + half]
                sc = st["s"][half][c * QBLK:(c + 1) * QBLK] + self.bias[qb][...]
                m = jnp.maximum(jnp.max(sc, axis=-1, keepdims=True), sink)
                e = jnp.exp2(sc - m)
                den = jnp.sum(e, axis=-1, keepdims=True) + jnp.exp2(sink - m)
                es.append(e.astype(BF16))
                inv.append(1.0 / den)
            o = jnp.dot(jnp.concatenate(es, axis=1), v2, preferred_element_type=F32)
            o = o * jnp.where(self.lo, inv[0], inv[1])
            gate = self._zcol(COL_GA + p * LANES, qb)
            self.mix_ref[qb * QBLK:(qb + 1) * QBLK, p * LANES:(p + 1) * LANES] = (
                o * _silu(gate)).astype(BF16)

    def carry(self):
        self.kband[:, :, 0:QBLK] = self.kband[:, :, self.tm:self.tm + QBLK]
        self.vband[:, 0:QBLK, :] = self.vband[:, self.tm:self.tm + QBLK, :]


def _gated_conv(zc_ref, cw, u_scr, rows, n_rows):
    seg = lambda col: zc_ref[rows, col:col + CONV_WIDTH]
    u = seg(ZC_C) * seg(ZC_H)
    r0 = 8 + rows.start
    u_scr[r0:r0 + n_rows, :] = u
    conv = cw[0:1] * u_scr[r0 - 2:r0 - 2 + n_rows, :] + cw[1:2] * u_scr[r0 - 1:r0 - 1 + n_rows, :] + cw[2:3] * u
    gc = seg(ZC_GC)
    return seg(ZC_B) * conv * _silu(gc)


def _front_kernel(sinks_ref, x_ref, g_ref, w_ref, rope_ref, bias_ref, qg_ref, kg_ref,
                  mix_ref, zc_ref, h0, h1, z0, z1, kband, vband, *, tm, blocks_per_seq):
    i = pl.program_id(0)
    first_of_seq = (i + 2 * blocks_per_seq - 2) % blocks_per_seq == 0

    @pl.when(i == 0)
    def _():
        h1[...] = jnp.zeros_like(h1)
        z0[...] = jnp.zeros_like(z0)
        kband[...] = jnp.zeros_like(kband)
        vband[...] = jnp.zeros_like(vband)

    @pl.when(first_of_seq)
    def _():
        kband[:, :, 0:QBLK] = jnp.zeros((N_KV_HEADS, HEAD_DIM, QBLK), BF16)
        vband[:, 0:QBLK, :] = jnp.zeros((N_KV_HEADS, QBLK, LANES), BF16)

    def body(h_w, h_r, z_w, z_r):
        def norm_piece(r0, r1):
            x = x_ref[r0:r1, :]
            ms = jnp.mean(x * x, axis=-1, keepdims=True)
            h_w[r0:r1, :] = (x * lax.rsqrt(ms + EPS) * g_ref[...]).astype(BF16)

        rw_ = tm // FILL_PIECES
        fillers = iter([functools.partial(norm_piece, j * rw_, (j + 1) * rw_) for j in range(FILL_PIECES)])
        chunks = iter(range(IN_WIDTH // PROJ_CHUNK))

        def project(count):
            for n in itertools.islice(chunks, count):
                res = jnp.dot(h_r[...], w_ref[:, n * PROJ_CHUNK:(n + 1) * PROJ_CHUNK], preferred_element_type=F32)
                if n < ATTN_COLS // PROJ_CHUNK:
                    z_w[:, n * PROJ_CHUNK:(n + 1) * PROJ_CHUNK] = res
                else:
                    zc_ref[:, n * PROJ_CHUNK - ATTN_COLS:(n + 1) * PROJ_CHUNK - ATTN_COLS] = res

        def fill(count):
            for f in itertools.islice(fillers, count):
                f()

        att = _Attention(sinks_ref, z_r, rope_ref, bias_ref, qg_ref[...], kg_ref[...],
                         mix_ref, kband, vband, first_of_seq, tm)
        n_units = len(att.units)
        att.sumsq(0)
        project(1)
        for u in range(n_units):
            if u > 0:
                att.output(u - 1)
            att.scores(u)
            if u + 1 < n_units:
                att.sumsq(u + 1)
            project(1)
        att.output(n_units - 1)
        att.carry()
        project(1)
        for _ in range(TAIL_CHUNKS):
            project(1)
        fill(FILL_PIECES)

    @pl.when(i % 2 == 0)
    def _():
        body(h0, h1, z1, z0)

    @pl.when(i % 2 == 1)
    def _():
        body(h1, h0, z0, z1)


def _front(x2, gain, w_in, sinks, tabs, bias, qg, kg, seq, tm):
    t = x2.shape[0]
    nblk = t // tm
    bps = seq // tm
    const = lambda r, c: pl.BlockSpec((r, c), lambda i: (0, 0), pipeline_mode=pl.Buffered(1))
    tab = pl.BlockSpec((tm, 3 * LANES), lambda i: ((i + 2 * bps - 2) % bps, 0))
    return pl.pallas_call(
        functools.partial(_front_kernel, tm=tm, blocks_per_seq=bps),
        grid=(nblk + 2,),
        in_specs=[
            pl.BlockSpec(memory_space=pltpu.SMEM),
            pl.BlockSpec((tm, D_MODEL), lambda i: (jnp.minimum(i, nblk - 1), 0)),
            const(1, D_MODEL),
            const(D_MODEL, IN_WIDTH),
            tab,
            pl.BlockSpec((2, QBLK, 2 * QBLK), lambda i: (0, 0, 0), pipeline_mode=pl.Buffered(1)),
            const(1, LANES), const(1, LANES),
        ],
        out_specs=[
            pl.BlockSpec((tm, ATTN_WIDTH), lambda i: (jnp.maximum(i - 2, 0), 0)),
            pl.BlockSpec((tm, IN_WIDTH - ATTN_COLS), lambda i: (jnp.clip(i - 1, 0, nblk - 1), 0)),
        ],
        out_shape=[
            jax.ShapeDtypeStruct((t, ATTN_WIDTH), BF16),
            jax.ShapeDtypeStruct((t, IN_WIDTH - ATTN_COLS), F32),
        ],
        scratch_shapes=[
            pltpu.VMEM((tm, D_MODEL), BF16),
            pltpu.VMEM((tm, D_MODEL), BF16),
            pltpu.VMEM((tm, ATTN_COLS), F32),
            pltpu.VMEM((tm, ATTN_COLS), F32),
            pltpu.VMEM((N_KV_HEADS, HEAD_DIM, tm + QBLK), BF16),
            pltpu.VMEM((N_KV_HEADS, tm + QBLK, LANES), BF16),
        ],
        compiler_params=pltpu.CompilerParams(
            dimension_semantics=("arbitrary",),
            vmem_limit_bytes=58 * 1024 * 1024),
        name="front",
    )(sinks, x2, gain, w_in, tabs, bias, qg, kg)


def _out_kernel(mixa_ref, zc_ref, x_ref, p_ref, cw_ref, wo_ref, g2_ref, wg_ref, bg_ref, wp_ref, g3_ref,
                o_ref, u_scr, *, blocks_per_seq):
    tm = o_ref.shape[0]

    @pl.when(pl.program_id(0) % blocks_per_seq == 0)
    def _():
        u_scr[0:8, :] = jnp.zeros((8, CONV_WIDTH), F32)

    halves = [slice(r * (tm // 2), (r + 1) * (tm // 2)) for r in range(2)]
    cw = cw_ref[...]
    pes = [jnp.dot(p_ref[rows, :].astype(BF16), wp_ref[...], preferred_element_type=F32) for rows in halves]
    for rows in halves:
        y_conv = _gated_conv(zc_ref, cw, u_scr, rows, tm // 2).astype(BF16)
        mix = jnp.concatenate([mixa_ref[rows, :], y_conv], axis=1)
        o_ref[rows, :] = x_ref[rows, :] + jnp.dot(mix, wo_ref[...], preferred_element_type=F32)
    u_scr[0:8, :] = u_scr[tm:tm + 8, :]
    for rows, pe in zip(halves, pes):
        x1 = o_ref[rows, :]
        ms = jnp.mean(x1 * x1, axis=-1, keepdims=True)
        hn = (x1 * lax.rsqrt(ms + EPS) * g2_ref[...]).astype(BF16)
        t = jnp.tanh(0.5 * (jnp.dot(hn, wg_ref[...], preferred_element_type=F32) + bg_ref[...]))
        e_half = pe * lax.rsqrt(jnp.mean(pe * pe, axis=-1, keepdims=True) + EPS) * (0.5 * g3_ref[...])
        o_ref[rows, :] = (x1 + e_half) + e_half * t


def _out_stage(mixa, zc, x2, p2, conv_w, w_out, g2, w_gate, b_gate, w_proj, g3, seq, tm):
    t = x2.shape[0]
    row = lambda w: pl.BlockSpec((tm, w), lambda i: (i, 0))
    const = lambda r, c: pl.BlockSpec((r, c), lambda i: (0, 0), pipeline_mode=pl.Buffered(1))
    return pl.pallas_call(
        functools.partial(_out_kernel, blocks_per_seq=seq // tm),
        grid=(t // tm,),
        in_specs=[
            row(ATTN_WIDTH), row(IN_WIDTH - ATTN_COLS), row(D_MODEL), row(PLE_DIM),
            const(CONV_K, CONV_WIDTH),
            const(D_MODEL, D_MODEL), const(1, D_MODEL),
            const(D_MODEL, D_MODEL), const(1, D_MODEL),
            const(PLE_DIM, D_MODEL), const(1, D_MODEL),
        ],
        out_specs=row(D_MODEL),
        out_shape=jax.ShapeDtypeStruct((t, D_MODEL), F32),
        scratch_shapes=[pltpu.VMEM((tm + 8, CONV_WIDTH), F32)],
        compiler_params=pltpu.CompilerParams(
            dimension_semantics=("arbitrary",),
            vmem_limit_bytes=62 * 1024 * 1024),
        name="outproj_ple",
    )(mixa, zc, x2, p2, conv_w, w_out, g2, w_gate, b_gate, w_proj, g3)


def _rope_tables(s):
    half = ROT_DIM // 2
    inv_freq = jnp.power(jnp.float32(ROPE_THETA), -jnp.arange(half, dtype=F32) * 2.0 / ROT_DIM)
    ang = jnp.arange(s).astype(F32)[:, None] * inv_freq[None, :]
    cos, sin = jnp.cos(ang), jnp.sin(ang)
    pad = lambda n: jnp.zeros((s, n), F32)
    c64 = jnp.concatenate([cos, cos, jnp.ones((s, HEAD_DIM - ROT_DIM), F32)], axis=1)
    s1 = jnp.concatenate([-sin, pad(HEAD_DIM - half)], axis=1)
    s2 = jnp.concatenate([pad(half), sin, pad(HEAD_DIM - ROT_DIM)], axis=1)
    return jnp.concatenate([jnp.tile(t, (1, LANES // HEAD_DIM)) for t in (c64, s1, s2)], axis=1)


def _band_bias():
    qi = jnp.arange(QBLK)[:, None]
    ci = jnp.arange(2 * QBLK)[None, :]
    band = (ci > qi) & (ci <= qi + QBLK)
    masks = jnp.stack([band, band & (ci >= QBLK)])
    return jnp.where(masks, 0.0, NEG_INF).astype(F32)


def _layer(x, p_i, norm_gain, w_in, q_gain, k_gain, sinks, conv_w, w_out,
           gate_norm_gain, w_gate, b_gate, w_proj, ple_norm_gain):
    b, s, _ = x.shape
    t = b * s
    x2 = x.reshape(t, D_MODEL)
    qg = jnp.tile(q_gain * (HEAD_DIM ** -0.5 * LOG2E), LANES // HEAD_DIM).reshape(1, LANES)
    kg = jnp.tile(k_gain, LANES // HEAD_DIM).reshape(1, LANES)
    mixa, zc = _front(x2, norm_gain.reshape(1, D_MODEL), w_in.astype(BF16), sinks * LOG2E, _rope_tables(s), _band_bias(), qg, kg,
                      s, min(FRONT_TM, s))

    out = _out_stage(mixa, zc, x2, p_i.reshape(t, PLE_DIM), conv_w, w_out.astype(BF16),
                     gate_norm_gain.reshape(1, D_MODEL), w_gate.astype(BF16),
                     b_gate.reshape(1, D_MODEL), w_proj.astype(BF16),
                     ple_norm_gain.reshape(1, D_MODEL), s, min(OUT_TM, s))
    return out.reshape(b, s, D_MODEL)


def kernel(x, p, norm_gain, w_in, q_norm_gain, k_norm_gain, attn_sinks, conv_w, w_out,
           ple_gate_norm_gain, w_ple_gate, b_ple_gate, w_ple_proj, ple_norm_gain):
    for i in range(p.shape[0]):
        x = _layer(x, p[i], norm_gain[i], w_in[i], q_norm_gain[i], k_norm_gain[i], attn_sinks[i],
                   conv_w[i], w_out[i], ple_gate_norm_gain[i], w_ple_gate[i], b_ple_gate[i],
                   w_ple_proj[i], ple_norm_gain[i])
    return x
```

```python
import functools
import itertools

import jax
import jax.numpy as jnp
from jax import lax
from jax.experimental import pallas as pl
from jax.experimental.pallas import tpu as pltpu

D_MODEL = 2048
PLE_DIM = 256
ATTN_WIDTH = 1024
CONV_WIDTH = 1024
HEAD_DIM = 64
N_Q_HEADS = 16
N_KV_HEADS = 4
KV_WIDTH = 256
WINDOW = 128
ROT_DIM = 16
ROPE_THETA = 500000.0
CONV_K = 3
EPS = 1e-6
NEG_INF = -1e30
IN_WIDTH = 2 * ATTN_WIDTH + 2 * KV_WIDTH + 4 * CONV_WIDTH

LANES = 128
SUBLANES = 8
VMEM_BYTES = 64 * 1024 * 1024
QBLK = 128

COL_Q, COL_K, COL_V, COL_GA, COL_B, COL_C, COL_H, COL_GC = 0, 1024, 1280, 1536, 2560, 3584, 4608, 5632
ATTN_COLS = COL_B
ZC_B, ZC_C, ZC_H, ZC_GC = (c - COL_B for c in (COL_B, COL_C, COL_H, COL_GC))
PROJ_CHUNK = 512
FRONT_TM = 256
FILL_PIECES = 8
TAIL_CHUNKS = 3
OUT_TM = 512
FRONT_VMEM_LIMIT = VMEM_BYTES - 6 * 1024 * 1024
OUT_VMEM_LIMIT = VMEM_BYTES - 2 * 1024 * 1024

F32 = jnp.float32
BF16 = jnp.bfloat16
LOG2E = 1.4426950408889634


def _silu(g):
    half = 0.5 * g
    return half + half * jnp.tanh(half)


class _Attention:
    def __init__(self, sinks_ref, z_ref, tabs, bias_ref, qg, kg, mix_ref, kband, vband, first_of_seq, tm):
        self.sinks_ref, self.z_ref, self.tabs, self.qg, self.kg = sinks_ref, z_ref, tabs, qg, kg
        self.mix_ref, self.kband, self.vband, self.tm = mix_ref, kband, vband, tm
        self.units = [(qb, g) for qb in range(tm // QBLK) for g in range(N_KV_HEADS)]
        self.lo = lax.broadcasted_iota(jnp.int32, (1, LANES), 1) < HEAD_DIM
        rr = (lax.broadcasted_iota(jnp.int32, (2 * LANES, LANES), 0) % LANES) // HEAD_DIM
        cc = lax.broadcasted_iota(jnp.int32, (2 * LANES, LANES), 1) // HEAD_DIM
        self.bd = jnp.where(rr == cc, 1.0, 0.0).astype(BF16)
        self.bias = [bias_ref.at[first_of_seq.astype(jnp.int32)]] + [bias_ref.at[0]] * (tm // QBLK - 1)
        self.state = {}

    def _zcol(self, col, qb):
        return self.z_ref[qb * QBLK:(qb + 1) * QBLK, col:col + LANES]

    def _head_sumsq(self, t):
        sq = t * t
        hi = sq.astype(BF16)
        rest = (sq - hi.astype(F32)).astype(BF16)
        return jnp.dot(jnp.concatenate([hi, rest], axis=1), self.bd, preferred_element_type=F32)

    def _head_norm_rope(self, t, ss, gain, qb):
        rc, rs1, rs2 = (self.tabs[qb * QBLK:(qb + 1) * QBLK, j * LANES:(j + 1) * LANES] for j in range(3))
        y = t * lax.rsqrt(ss * (1.0 / HEAD_DIM) + EPS) * gain
        return y * rc + pltpu.roll(y, LANES - ROT_DIM // 2, 1) * rs1 + pltpu.roll(y, ROT_DIM // 2, 1) * rs2

    def _both_halves(self, t):
        tr = pltpu.roll(t, HEAD_DIM, 1)
        return jnp.where(self.lo, t, tr), jnp.where(self.lo, tr, t)

    def sumsq(self, u):
        qb, g = self.units[u]
        st = self.state[u] = {}
        if g == 0:
            st["k_raw"] = [self._zcol(COL_K + j * LANES, qb) for j in range(KV_WIDTH // LANES)]
            st["k_ss"] = [self._head_sumsq(t) for t in st["k_raw"]]
        st["q_raw"] = [self._zcol(COL_Q + (2 * g + c) * LANES, qb) for c in range(2)]
        st["q_ss"] = [self._head_sumsq(t) for t in st["q_raw"]]

    def scores(self, u):
        qb, g = self.units[u]
        st = self.state[u]
        rows = slice(QBLK + qb * QBLK, QBLK + (qb + 1) * QBLK)
        if g == 0:
            for j in range(KV_WIDTH // LANES):
                kt = self._head_norm_rope(st["k_raw"][j], st["k_ss"][j], self.kg, qb).T
                for half in range(2):
                    self.kband[2 * j + half, :, rows] = kt[half * HEAD_DIM:(half + 1) * HEAD_DIM].astype(BF16)
                for half, t in enumerate(self._both_halves(self._zcol(COL_V + j * LANES, qb))):
                    self.vband[2 * j + half, rows, :] = t.astype(BF16)
        q = jnp.concatenate(
            [self._head_norm_rope(t, ss, self.qg, qb).astype(BF16) for t, ss in zip(st["q_raw"], st["q_ss"])],
            axis=0)
        kt = self.kband[g, :, qb * QBLK:(qb + 2) * QBLK]
        zero = jnp.zeros_like(kt)
        st["s"] = [jnp.dot(q, jnp.concatenate([kt, zero], axis=0), preferred_element_type=F32),
                   jnp.dot(q, jnp.concatenate([zero, kt], axis=0), preferred_element_type=F32)]

    def output(self, u):
        qb, g = self.units[u]
        st = self.state.pop(u)
        vt = self.vband[g, qb * QBLK:(qb + 2) * QBLK, :]
        zero = jnp.zeros_like(vt)
        v2 = jnp.concatenate([jnp.where(self.lo, vt, zero), jnp.where(self.lo, zero, vt)], axis=0)
        for c in range(2):
            p = 2 * g + c
            es, inv = [], []
            for half in range(2):
                sink = self.sinks_ref[2 * p + half]
                sc = st["s"][half][c * QBLK:(c + 1) * QBLK] + self.bias[qb][...]
                m = jnp.maximum(jnp.max(sc, axis=-1, keepdims=True), sink)
                e = jnp.exp2(sc - m)
                den = jnp.sum(e, axis=-1, keepdims=True) + jnp.exp2(sink - m)
                es.append(e.astype(BF16))
                inv.append(1.0 / den)
            o = jnp.dot(jnp.concatenate(es, axis=1), v2, preferred_element_type=F32)
            o = o * jnp.where(self.lo, inv[0], inv[1])
            gate = self._zcol(COL_GA + p * LANES, qb)
            self.mix_ref[qb * QBLK:(qb + 1) * QBLK, p * LANES:(p + 1) * LANES] = (
                o * _silu(gate)).astype(BF16)

    def carry(self):
        self.kband[:, :, 0:QBLK] = self.kband[:, :, self.tm:self.tm + QBLK]
        self.vband[:, 0:QBLK, :] = self.vband[:, self.tm:self.tm + QBLK, :]


def _gated_conv(zc_ref, cw, u_scr, rows, n_rows):
    seg = lambda col: zc_ref[rows, col:col + CONV_WIDTH]
    u = seg(ZC_C) * seg(ZC_H)
    r0 = SUBLANES + rows.start
    u_scr[r0:r0 + n_rows, :] = u
    ext = u_scr[r0 - SUBLANES:r0 + n_rows, :]
    conv = (cw[0:1] * pltpu.roll(ext, 2, 0)[SUBLANES:, :] + cw[1:2] * pltpu.roll(ext, 1, 0)[SUBLANES:, :]
            + cw[2:3] * u)
    gc = seg(ZC_GC)
    return seg(ZC_B) * conv * _silu(gc)


def _front_kernel(sinks_ref, x_ref, g_ref, w_ref, rope_ref, bias_ref, qg_ref, kg_ref,
                  mix_ref, zc_ref, h0, h1, z0, z1, kband, vband, *, tm, blocks_per_seq):
    i = pl.program_id(0)
    first_of_seq = (i + 2 * blocks_per_seq - 2) % blocks_per_seq == 0

    @pl.when(i == 0)
    def _():
        h1[...] = jnp.zeros_like(h1)
        z0[...] = jnp.zeros_like(z0)
        kband[...] = jnp.zeros_like(kband)
        vband[...] = jnp.zeros_like(vband)

    @pl.when(first_of_seq)
    def _():
        kband[:, :, 0:QBLK] = jnp.zeros((N_KV_HEADS, HEAD_DIM, QBLK), BF16)
        vband[:, 0:QBLK, :] = jnp.zeros((N_KV_HEADS, QBLK, LANES), BF16)

    def body(h_w, h_r, z_w, z_r):
        def norm_piece(r0, r1):
            x = x_ref[r0:r1, :]
            ms = jnp.mean(x * x, axis=-1, keepdims=True)
            h_w[r0:r1, :] = (x * lax.rsqrt(ms + EPS) * g_ref[...]).astype(BF16)

        rw_ = tm // FILL_PIECES
        fillers = iter([functools.partial(norm_piece, j * rw_, (j + 1) * rw_) for j in range(FILL_PIECES)])
        chunks = iter(range(IN_WIDTH // PROJ_CHUNK))

        def project(count):
            for n in itertools.islice(chunks, count):
                res = jnp.dot(h_r[...], w_ref[:, n * PROJ_CHUNK:(n + 1) * PROJ_CHUNK], preferred_element_type=F32)
                if n < ATTN_COLS // PROJ_CHUNK:
                    z_w[:, n * PROJ_CHUNK:(n + 1) * PROJ_CHUNK] = res
                else:
                    zc_ref[:, n * PROJ_CHUNK - ATTN_COLS:(n + 1) * PROJ_CHUNK - ATTN_COLS] = res

        def fill(count):
            for f in itertools.islice(fillers, count):
                f()

        att = _Attention(sinks_ref, z_r, rope_ref, bias_ref, qg_ref[...], kg_ref[...],
                         mix_ref, kband, vband, first_of_seq, tm)
        n_units = len(att.units)
        att.sumsq(0)
        project(1)
        for u in range(n_units):
            if u > 0:
                att.output(u - 1)
            att.scores(u)
            if u + 1 < n_units:
                att.sumsq(u + 1)
            project(1)
        att.output(n_units - 1)
        att.carry()
        project(1)
        for _ in range(TAIL_CHUNKS):
            project(1)
        fill(FILL_PIECES)

    @pl.when(i % 2 == 0)
    def _():
        body(h0, h1, z1, z0)

    @pl.when(i % 2 == 1)
    def _():
        body(h1, h0, z0, z1)


def _front(x2, gain, w_in, sinks, tabs, bias, qg, kg, seq, tm):
    t = x2.shape[0]
    nblk = t // tm
    bps = seq // tm
    const = lambda r, c: pl.BlockSpec((r, c), lambda i: (0, 0), pipeline_mode=pl.Buffered(1))
    tab = pl.BlockSpec((tm, 3 * LANES), lambda i: ((i + 2 * bps - 2) % bps, 0))
    return pl.pallas_call(
        functools.partial(_front_kernel, tm=tm, blocks_per_seq=bps),
        grid=(nblk + 2,),
        in_specs=[
            pl.BlockSpec(memory_space=pltpu.SMEM),
            pl.BlockSpec((tm, D_MODEL), lambda i: (jnp.minimum(i, nblk - 1), 0)),
            const(1, D_MODEL),
            const(D_MODEL, IN_WIDTH),
            tab,
            pl.BlockSpec((2, QBLK, 2 * QBLK), lambda i: (0, 0, 0), pipeline_mode=pl.Buffered(1)),
            const(1, LANES), const(1, LANES),
        ],
        out_specs=[
            pl.BlockSpec((tm, ATTN_WIDTH), lambda i: (jnp.maximum(i - 2, 0), 0)),
            pl.BlockSpec((tm, IN_WIDTH - ATTN_COLS), lambda i: (jnp.clip(i - 1, 0, nblk - 1), 0)),
        ],
        out_shape=[
            jax.ShapeDtypeStruct((t, ATTN_WIDTH), BF16),
            jax.ShapeDtypeStruct((t, IN_WIDTH - ATTN_COLS), F32),
        ],
        scratch_shapes=[
            pltpu.VMEM((tm, D_MODEL), BF16),
            pltpu.VMEM((tm, D_MODEL), BF16),
            pltpu.VMEM((tm, ATTN_COLS), F32),
            pltpu.VMEM((tm, ATTN_COLS), F32),
            pltpu.VMEM((N_KV_HEADS, HEAD_DIM, tm + QBLK), BF16),
            pltpu.VMEM((N_KV_HEADS, tm + QBLK, LANES), BF16),
        ],
        compiler_params=pltpu.CompilerParams(
            dimension_semantics=("arbitrary",),
            vmem_limit_bytes=FRONT_VMEM_LIMIT),
        name="front",
    )(sinks, x2, gain, w_in, tabs, bias, qg, kg)


def _out_kernel(mixa_ref, zc_ref, x_ref, p_ref, cw_ref, wo_ref, g2_ref, wg_ref, bg_ref, wp_ref, g3_ref,
                o_ref, u_scr, *, blocks_per_seq):
    tm = o_ref.shape[0]

    @pl.when(pl.program_id(0) % blocks_per_seq == 0)
    def _():
        u_scr[0:SUBLANES, :] = jnp.zeros((SUBLANES, CONV_WIDTH), F32)

    halves = [slice(r * (tm // 2), (r + 1) * (tm // 2)) for r in range(2)]
    cw = cw_ref[...]
    pes = [jnp.dot(p_ref[rows, :].astype(BF16), wp_ref[...], preferred_element_type=F32) for rows in halves]
    for rows in halves:
        y_conv = _gated_conv(zc_ref, cw, u_scr, rows, tm // 2).astype(BF16)
        mix = jnp.concatenate([mixa_ref[rows, :], y_conv], axis=1)
        o_ref[rows, :] = x_ref[rows, :] + jnp.dot(mix, wo_ref[...], preferred_element_type=F32)
    u_scr[0:SUBLANES, :] = u_scr[tm:tm + SUBLANES, :]
    for rows, pe in zip(halves, pes):
        x1 = o_ref[rows, :]
        ms = jnp.mean(x1 * x1, axis=-1, keepdims=True)
        hn = (x1 * lax.rsqrt(ms + EPS) * g2_ref[...]).astype(BF16)
        t = jnp.tanh(0.5 * (jnp.dot(hn, wg_ref[...], preferred_element_type=F32) + bg_ref[...]))
        e_half = pe * lax.rsqrt(jnp.mean(pe * pe, axis=-1, keepdims=True) + EPS) * (0.5 * g3_ref[...])
        o_ref[rows, :] = (x1 + e_half) + e_half * t


def _out_stage(mixa, zc, x2, p2, conv_w, w_out, g2, w_gate, b_gate, w_proj, g3, seq, tm):
    t = x2.shape[0]
    row = lambda w: pl.BlockSpec((tm, w), lambda i: (i, 0))
    const = lambda r, c: pl.BlockSpec((r, c), lambda i: (0, 0), pipeline_mode=pl.Buffered(1))
    return pl.pallas_call(
        functools.partial(_out_kernel, blocks_per_seq=seq // tm),
        grid=(t // tm,),
        in_specs=[
            row(ATTN_WIDTH), row(IN_WIDTH - ATTN_COLS), row(D_MODEL), row(PLE_DIM),
            const(CONV_K, CONV_WIDTH),
            const(D_MODEL, D_MODEL), const(1, D_MODEL),
            const(D_MODEL, D_MODEL), const(1, D_MODEL),
            const(PLE_DIM, D_MODEL), const(1, D_MODEL),
        ],
        out_specs=row(D_MODEL),
        out_shape=jax.ShapeDtypeStruct((t, D_MODEL), F32),
        scratch_shapes=[pltpu.VMEM((tm + SUBLANES, CONV_WIDTH), F32)],
        compiler_params=pltpu.CompilerParams(
            dimension_semantics=("arbitrary",),
            vmem_limit_bytes=OUT_VMEM_LIMIT),
        name="outproj_ple",
    )(mixa, zc, x2, p2, conv_w, w_out, g2, w_gate, b_gate, w_proj, g3)


def _rope_tables(s):
    half = ROT_DIM // 2
    inv_freq = jnp.power(jnp.float32(ROPE_THETA), -jnp.arange(half, dtype=F32) * 2.0 / ROT_DIM)
    ang = jnp.arange(s).astype(F32)[:, None] * inv_freq[None, :]
    cos, sin = jnp.cos(ang), jnp.sin(ang)
    pad = lambda n: jnp.zeros((s, n), F32)
    c64 = jnp.concatenate([cos, cos, jnp.ones((s, HEAD_DIM - ROT_DIM), F32)], axis=1)
    s1 = jnp.concatenate([-sin, pad(HEAD_DIM - half)], axis=1)
    s2 = jnp.concatenate([pad(half), sin, pad(HEAD_DIM - ROT_DIM)], axis=1)
    return jnp.concatenate([jnp.tile(t, (1, LANES // HEAD_DIM)) for t in (c64, s1, s2)], axis=1)


def _band_bias():
    qi = jnp.arange(QBLK)[:, None]
    ci = jnp.arange(2 * QBLK)[None, :]
    band = (ci > qi) & (ci <= qi + QBLK)
    masks = jnp.stack([band, band & (ci >= QBLK)])
    return jnp.where(masks, 0.0, NEG_INF).astype(F32)


def _layer(x, p_i, norm_gain, w_in, q_gain, k_gain, sinks, conv_w, w_out,
           gate_norm_gain, w_gate, b_gate, w_proj, ple_norm_gain):
    b, s, _ = x.shape
    t = b * s
    x2 = x.reshape(t, D_MODEL)
    qg = jnp.tile(q_gain * (HEAD_DIM ** -0.5 * LOG2E), LANES // HEAD_DIM).reshape(1, LANES)
    kg = jnp.tile(k_gain, LANES // HEAD_DIM).reshape(1, LANES)
    mixa, zc = _front(x2, norm_gain.reshape(1, D_MODEL), w_in.astype(BF16), sinks * LOG2E, _rope_tables(s), _band_bias(), qg, kg,
                      s, min(FRONT_TM, s))

    out = _out_stage(mixa, zc, x2, p_i.reshape(t, PLE_DIM), conv_w, w_out.astype(BF16),
                     gate_norm_gain.reshape(1, D_MODEL), w_gate.astype(BF16),
                     b_gate.reshape(1, D_MODEL), w_proj.astype(BF16),
                     ple_norm_gain.reshape(1, D_MODEL), s, min(OUT_TM, s))
    return out.reshape(b, s, D_MODEL)


def kernel(x, p, norm_gain, w_in, q_norm_gain, k_norm_gain, attn_sinks, conv_w, w_out,
           ple_gate_norm_gain, w_ple_gate, b_ple_gate, w_ple_proj, ple_norm_gain):
    for i in range(p.shape[0]):
        x = _layer(x, p[i], norm_gain[i], w_in[i], q_norm_gain[i], k_norm_gain[i], attn_sinks[i],
                   conv_w[i], w_out[i], ple_gate_norm_gain[i], w_ple_gate[i], b_ple_gate[i],
                   w_ple_proj[i], ple_norm_gain[i])
    return x
```

```python
import functools
import itertools

import jax
import jax.numpy as jnp
from jax import lax
from jax.experimental import pallas as pl
from jax.experimental.pallas import tpu as pltpu

D_MODEL = 2048
PLE_DIM = 256
ATTN_WIDTH = 1024
CONV_WIDTH = 1024
HEAD_DIM = 64
N_Q_HEADS = 16
N_KV_HEADS = 4
KV_WIDTH = 256
WINDOW = 128
ROT_DIM = 16
ROPE_THETA = 500000.0
CONV_K = 3
EPS = 1e-6
NEG_INF = -1e30
IN_WIDTH = 2 * ATTN_WIDTH + 2 * KV_WIDTH + 4 * CONV_WIDTH

LANES = 128
SUBLANES = 8
VMEM_BYTES = 64 * 1024 * 1024
QBLK = 128

COL_Q, COL_K, COL_V, COL_GA, COL_B, COL_C, COL_H, COL_GC = 0, 1024, 1280, 1536, 2560, 3584, 4608, 5632
ATTN_COLS = COL_B
ZC_B, ZC_C, ZC_H, ZC_GC = (c - COL_B for c in (COL_B, COL_C, COL_H, COL_GC))
PROJ_CHUNK = 512
FRONT_TM = 256
FILL_PIECES = 8
TAIL_CHUNKS = 3
OUT_TM = 512
FRONT_VMEM_LIMIT = VMEM_BYTES - 6 * 1024 * 1024
OUT_VMEM_LIMIT = VMEM_BYTES - 2 * 1024 * 1024

F32 = jnp.float32
BF16 = jnp.bfloat16
LOG2E = 1.4426950408889634


def _silu_of_half(half):
    return half + half * jnp.tanh(half)


class _Attention:
    def __init__(self, sinks_ref, z_ref, tabs, bias_ref, qg, kg, mix_ref, kband, vband, first_of_seq, tm):
        self.sinks_ref, self.z_ref, self.tabs, self.qg, self.kg = sinks_ref, z_ref, tabs, qg, kg
        self.mix_ref, self.kband, self.vband, self.tm = mix_ref, kband, vband, tm
        self.units = [(qb, g) for qb in range(tm // QBLK) for g in range(N_KV_HEADS)]
        self.lo = lax.broadcasted_iota(jnp.int32, (1, LANES), 1) < HEAD_DIM
        rr = (lax.broadcasted_iota(jnp.int32, (2 * LANES, LANES), 0) % LANES) // HEAD_DIM
        cc = lax.broadcasted_iota(jnp.int32, (2 * LANES, LANES), 1) // HEAD_DIM
        self.bd = jnp.where(rr == cc, 1.0, 0.0).astype(BF16)
        self.bias = [bias_ref.at[first_of_seq.astype(jnp.int32)]] + [bias_ref.at[0]] * (tm // QBLK - 1)
        self.state = {}

    def _zcol(self, col, qb):
        return self.z_ref[qb * QBLK:(qb + 1) * QBLK, col:col + LANES]

    def _head_sumsq(self, t):
        sq = t * t
        hi = sq.astype(BF16)
        rest = (sq - hi.astype(F32)).astype(BF16)
        return jnp.dot(jnp.concatenate([hi, rest], axis=1), self.bd, preferred_element_type=F32)

    def _head_norm_rope(self, t, ss, gain, qb):
        rc, rs1, rs2 = (self.tabs[qb * QBLK:(qb + 1) * QBLK, j * LANES:(j + 1) * LANES] for j in range(3))
        y = t * lax.rsqrt(ss * (1.0 / HEAD_DIM) + EPS) * gain
        return y * rc + pltpu.roll(y, LANES - ROT_DIM // 2, 1) * rs1 + pltpu.roll(y, ROT_DIM // 2, 1) * rs2

    def _both_halves(self, t):
        tr = pltpu.roll(t, HEAD_DIM, 1)
        return jnp.where(self.lo, t, tr), jnp.where(self.lo, tr, t)

    def sumsq(self, u):
        qb, g = self.units[u]
        st = self.state[u] = {}
        if g == 0:
            st["k_raw"] = [self._zcol(COL_K + j * LANES, qb) for j in range(KV_WIDTH // LANES)]
            st["k_ss"] = [self._head_sumsq(t) for t in st["k_raw"]]
        st["q_raw"] = [self._zcol(COL_Q + (2 * g + c) * LANES, qb) for c in range(2)]
        st["q_ss"] = [self._head_sumsq(t) for t in st["q_raw"]]

    def scores(self, u):
        qb, g = self.units[u]
        st = self.state[u]
        rows = slice(QBLK + qb * QBLK, QBLK + (qb + 1) * QBLK)
        if g == 0:
            for j in range(KV_WIDTH // LANES):
                kt = self._head_norm_rope(st["k_raw"][j], st["k_ss"][j], self.kg, qb).T
                for half in range(2):
                    self.kband[2 * j + half, :, rows] = kt[half * HEAD_DIM:(half + 1) * HEAD_DIM].astype(BF16)
                for half, t in enumerate(self._both_halves(self._zcol(COL_V + j * LANES, qb))):
                    self.vband[2 * j + half, rows, :] = t.astype(BF16)
        q = jnp.concatenate(
            [self._head_norm_rope(t, ss, self.qg, qb).astype(BF16) for t, ss in zip(st["q_raw"], st["q_ss"])],
            axis=0)
        kt = self.kband[g, :, qb * QBLK:(qb + 2) * QBLK]
        zero = jnp.zeros_like(kt)
        st["s"] = [jnp.dot(q, jnp.concatenate([kt, zero], axis=0), preferred_element_type=F32),
                   jnp.dot(q, jnp.concatenate([zero, kt], axis=0), preferred_element_type=F32)]

    def output(self, u):
        qb, g = self.units[u]
        st = self.state.pop(u)
        vt = self.vband[g, qb * QBLK:(qb + 2) * QBLK, :]
        zero = jnp.zeros_like(vt)
        v2 = jnp.concatenate([jnp.where(self.lo, vt, zero), jnp.where(self.lo, zero, vt)], axis=0)
        for c in range(2):
            p = 2 * g + c
            es, inv = [], []
            for half in range(2):
                sink = self.sinks_ref[2 * p + half]
                sc = st["s"][half][c * QBLK:(c + 1) * QBLK] + self.bias[qb][...]
                m = jnp.maximum(jnp.max(sc, axis=-1, keepdims=True), sink)
                e = jnp.exp2(sc - m)
                den = jnp.sum(e, axis=-1, keepdims=True) + jnp.exp2(sink - m)
                es.append(e.astype(BF16))
                inv.append(1.0 / den)
            o = jnp.dot(jnp.concatenate(es, axis=1), v2, preferred_element_type=F32)
            o = o * jnp.where(self.lo, inv[0], inv[1])
            gate = self._zcol(COL_GA + p * LANES, qb)
            self.mix_ref[qb * QBLK:(qb + 1) * QBLK, p * LANES:(p + 1) * LANES] = (
                o * _silu_of_half(gate)).astype(BF16)

    def carry(self):
        self.kband[:, :, 0:QBLK] = self.kband[:, :, self.tm:self.tm + QBLK]
        self.vband[:, 0:QBLK, :] = self.vband[:, self.tm:self.tm + QBLK, :]


def _gated_conv(zc_ref, cw, u_scr, rows, n_rows):
    seg = lambda col: zc_ref[rows, col:col + CONV_WIDTH]
    u = seg(ZC_C) * seg(ZC_H)
    r0 = SUBLANES + rows.start
    u_scr[r0:r0 + n_rows, :] = u
    ext = u_scr[r0 - SUBLANES:r0 + n_rows, :]
    conv = (cw[0:1] * pltpu.roll(ext, 2, 0)[SUBLANES:, :] + cw[1:2] * pltpu.roll(ext, 1, 0)[SUBLANES:, :]
            + cw[2:3] * u)
    gc = seg(ZC_GC)
    return seg(ZC_B) * conv * _silu_of_half(gc)


def _front_kernel(sinks_ref, x_ref, g_ref, w_ref, rope_ref, bias_ref, qg_ref, kg_ref,
                  mix_ref, zc_ref, h0, h1, z0, z1, kband, vband, *, tm, blocks_per_seq):
    i = pl.program_id(0)
    first_of_seq = (i + 2 * blocks_per_seq - 2) % blocks_per_seq == 0

    @pl.when(i == 0)
    def _():
        h1[...] = jnp.zeros_like(h1)
        z0[...] = jnp.zeros_like(z0)
        kband[...] = jnp.zeros_like(kband)
        vband[...] = jnp.zeros_like(vband)

    @pl.when(first_of_seq)
    def _():
        kband[:, :, 0:QBLK] = jnp.zeros((N_KV_HEADS, HEAD_DIM, QBLK), BF16)
        vband[:, 0:QBLK, :] = jnp.zeros((N_KV_HEADS, QBLK, LANES), BF16)

    def body(h_w, h_r, z_w, z_r):
        def norm_piece(r0, r1):
            x = x_ref[r0:r1, :]
            ms = jnp.mean(x * x, axis=-1, keepdims=True)
            h_w[r0:r1, :] = (x * lax.rsqrt(ms + EPS) * g_ref[...]).astype(BF16)

        rw_ = tm // FILL_PIECES
        fillers = iter([functools.partial(norm_piece, j * rw_, (j + 1) * rw_) for j in range(FILL_PIECES)])
        chunks = iter(range(IN_WIDTH // PROJ_CHUNK))

        def project(count):
            for n in itertools.islice(chunks, count):
                res = jnp.dot(h_r[...], w_ref[:, n * PROJ_CHUNK:(n + 1) * PROJ_CHUNK], preferred_element_type=F32)
                if n < ATTN_COLS // PROJ_CHUNK:
                    z_w[:, n * PROJ_CHUNK:(n + 1) * PROJ_CHUNK] = res
                else:
                    zc_ref[:, n * PROJ_CHUNK - ATTN_COLS:(n + 1) * PROJ_CHUNK - ATTN_COLS] = res

        def fill(count):
            for f in itertools.islice(fillers, count):
                f()

        att = _Attention(sinks_ref, z_r, rope_ref, bias_ref, qg_ref[...], kg_ref[...],
                         mix_ref, kband, vband, first_of_seq, tm)
        n_units = len(att.units)
        att.sumsq(0)
        project(1)
        for u in range(n_units):
            if u > 0:
                att.output(u - 1)
            att.scores(u)
            if u + 1 < n_units:
                att.sumsq(u + 1)
            project(1)
        att.output(n_units - 1)
        att.carry()
        project(1)
        for _ in range(TAIL_CHUNKS):
            project(1)
        fill(FILL_PIECES)

    @pl.when(i % 2 == 0)
    def _():
        body(h0, h1, z1, z0)

    @pl.when(i % 2 == 1)
    def _():
        body(h1, h0, z0, z1)


def _front(x2, gain, w_in, sinks, tabs, bias, qg, kg, seq, tm):
    t = x2.shape[0]
    nblk = t // tm
    bps = seq // tm
    const = lambda r, c: pl.BlockSpec((r, c), lambda i: (0, 0), pipeline_mode=pl.Buffered(1))
    tab = pl.BlockSpec((tm, 3 * LANES), lambda i: ((i + 2 * bps - 2) % bps, 0))
    return pl.pallas_call(
        functools.partial(_front_kernel, tm=tm, blocks_per_seq=bps),
        grid=(nblk + 2,),
        in_specs=[
            pl.BlockSpec(memory_space=pltpu.SMEM),
            pl.BlockSpec((tm, D_MODEL), lambda i: (jnp.minimum(i, nblk - 1), 0)),
            const(1, D_MODEL),
            const(D_MODEL, IN_WIDTH),
            tab,
            pl.BlockSpec((2, QBLK, 2 * QBLK), lambda i: (0, 0, 0), pipeline_mode=pl.Buffered(1)),
            const(1, LANES), const(1, LANES),
        ],
        out_specs=[
            pl.BlockSpec((tm, ATTN_WIDTH), lambda i: (jnp.maximum(i - 2, 0), 0)),
            pl.BlockSpec((tm, IN_WIDTH - ATTN_COLS), lambda i: (jnp.clip(i - 1, 0, nblk - 1), 0)),
        ],
        out_shape=[
            jax.ShapeDtypeStruct((t, ATTN_WIDTH), BF16),
            jax.ShapeDtypeStruct((t, IN_WIDTH - ATTN_COLS), F32),
        ],
        scratch_shapes=[
            pltpu.VMEM((tm, D_MODEL), BF16),
            pltpu.VMEM((tm, D_MODEL), BF16),
            pltpu.VMEM((tm, ATTN_COLS), F32),
            pltpu.VMEM((tm, ATTN_COLS), F32),
            pltpu.VMEM((N_KV_HEADS, HEAD_DIM, tm + QBLK), BF16),
            pltpu.VMEM((N_KV_HEADS, tm + QBLK, LANES), BF16),
        ],
        compiler_params=pltpu.CompilerParams(
            dimension_semantics=("arbitrary",),
            vmem_limit_bytes=FRONT_VMEM_LIMIT),
        name="front",
    )(sinks, x2, gain, w_in, tabs, bias, qg, kg)


def _out_kernel(mixa_ref, zc_ref, x_ref, p_ref, cw_ref, wo_ref, g2_ref, wg_ref, bg_ref, wp_ref, g3_ref,
                o_ref, u_scr, *, blocks_per_seq):
    tm = o_ref.shape[0]

    @pl.when(pl.program_id(0) % blocks_per_seq == 0)
    def _():
        u_scr[0:SUBLANES, :] = jnp.zeros((SUBLANES, CONV_WIDTH), F32)

    halves = [slice(r * (tm // 2), (r + 1) * (tm // 2)) for r in range(2)]
    cw = cw_ref[...]
    pes = [jnp.dot(p_ref[rows, :].astype(BF16), wp_ref[...], preferred_element_type=F32) for rows in halves]
    for rows in halves:
        y_conv = _gated_conv(zc_ref, cw, u_scr, rows, tm // 2).astype(BF16)
        mix = jnp.concatenate([mixa_ref[rows, :], y_conv], axis=1)
        o_ref[rows, :] = x_ref[rows, :] + jnp.dot(mix, wo_ref[...], preferred_element_type=F32)
    u_scr[0:SUBLANES, :] = u_scr[tm:tm + SUBLANES, :]
    for rows, pe in zip(halves, pes):
        x1 = o_ref[rows, :]
        ms = jnp.mean(x1 * x1, axis=-1, keepdims=True)
        hn = (x1 * lax.rsqrt(ms + EPS) * g2_ref[...]).astype(BF16)
        t = jnp.tanh(jnp.dot(hn, wg_ref[...], preferred_element_type=F32) + bg_ref[...])
        e_half = pe * lax.rsqrt(jnp.mean(pe * pe, axis=-1, keepdims=True) + EPS) * (0.5 * g3_ref[...])
        o_ref[rows, :] = (x1 + e_half) + e_half * t


def _out_stage(mixa, zc, x2, p2, conv_w, w_out, g2, w_gate, b_gate, w_proj, g3, seq, tm):
    t = x2.shape[0]
    row = lambda w: pl.BlockSpec((tm, w), lambda i: (i, 0))
    const = lambda r, c: pl.BlockSpec((r, c), lambda i: (0, 0), pipeline_mode=pl.Buffered(1))
    return pl.pallas_call(
        functools.partial(_out_kernel, blocks_per_seq=seq // tm),
        grid=(t // tm,),
        in_specs=[
            row(ATTN_WIDTH), row(IN_WIDTH - ATTN_COLS), row(D_MODEL), row(PLE_DIM),
            const(CONV_K, CONV_WIDTH),
            const(D_MODEL, D_MODEL), const(1, D_MODEL),
            const(D_MODEL, D_MODEL), const(1, D_MODEL),
            const(PLE_DIM, D_MODEL), const(1, D_MODEL),
        ],
        out_specs=row(D_MODEL),
        out_shape=jax.ShapeDtypeStruct((t, D_MODEL), F32),
        scratch_shapes=[pltpu.VMEM((tm + SUBLANES, CONV_WIDTH), F32)],
        compiler_params=pltpu.CompilerParams(
            dimension_semantics=("arbitrary",),
            vmem_limit_bytes=OUT_VMEM_LIMIT),
        name="outproj_ple",
    )(mixa, zc, x2, p2, conv_w, w_out, g2, w_gate, b_gate, w_proj, g3)


def _rope_tables(s):
    half = ROT_DIM // 2
    inv_freq = jnp.power(jnp.float32(ROPE_THETA), -jnp.arange(half, dtype=F32) * 2.0 / ROT_DIM)
    ang = jnp.arange(s).astype(F32)[:, None] * inv_freq[None, :]
    cos, sin = jnp.cos(ang), jnp.sin(ang)
    pad = lambda n: jnp.zeros((s, n), F32)
    c64 = jnp.concatenate([cos, cos, jnp.ones((s, HEAD_DIM - ROT_DIM), F32)], axis=1)
    s1 = jnp.concatenate([-sin, pad(HEAD_DIM - half)], axis=1)
    s2 = jnp.concatenate([pad(half), sin, pad(HEAD_DIM - ROT_DIM)], axis=1)
    return jnp.concatenate([jnp.tile(t, (1, LANES // HEAD_DIM)) for t in (c64, s1, s2)], axis=1)


def _band_bias():
    qi = jnp.arange(QBLK)[:, None]
    ci = jnp.arange(2 * QBLK)[None, :]
    band = (ci > qi) & (ci <= qi + QBLK)
    masks = jnp.stack([band, band & (ci >= QBLK)])
    return jnp.where(masks, 0.0, NEG_INF).astype(F32)


def _layer(x, p_i, norm_gain, w_in, q_gain, k_gain, sinks, conv_w, w_out,
           gate_norm_gain, w_gate, b_gate, w_proj, ple_norm_gain):
    b, s, _ = x.shape
    t = b * s
    x2 = x.reshape(t, D_MODEL)
    qg = jnp.tile(q_gain * (HEAD_DIM ** -0.5 * LOG2E), LANES // HEAD_DIM).reshape(1, LANES)
    kg = jnp.tile(k_gain, LANES // HEAD_DIM).reshape(1, LANES)
    halve = jnp.zeros((IN_WIDTH,), bool).at[COL_GA:COL_GA + ATTN_WIDTH].set(True).at[COL_GC:].set(True)
    w_in_b = jnp.where(halve[None, :], 0.5 * w_in, w_in).astype(BF16)
    mixa, zc = _front(x2, norm_gain.reshape(1, D_MODEL), w_in_b, sinks * LOG2E, _rope_tables(s), _band_bias(), qg, kg,
                      s, min(FRONT_TM, s))

    out = _out_stage(mixa, zc, x2, p_i.reshape(t, PLE_DIM), conv_w, w_out.astype(BF16),
                     gate_norm_gain.reshape(1, D_MODEL), (0.5 * w_gate).astype(BF16),
                     (0.5 * b_gate).reshape(1, D_MODEL), w_proj.astype(BF16),
                     ple_norm_gain.reshape(1, D_MODEL), s, min(OUT_TM, s))
    return out.reshape(b, s, D_MODEL)


def kernel(x, p, norm_gain, w_in, q_norm_gain, k_norm_gain, attn_sinks, conv_w, w_out,
           ple_gate_norm_gain, w_ple_gate, b_ple_gate, w_ple_proj, ple_norm_gain):
    for i in range(p.shape[0]):
        x = _layer(x, p[i], norm_gain[i], w_in[i], q_norm_gain[i], k_norm_gain[i], attn_sinks[i],
                   conv_w[i], w_out[i], ple_gate_norm_gain[i], w_ple_gate[i], b_ple_gate[i],
                   w_ple_proj[i], ple_norm_gain[i])
    return x
```

```python
import functools
import itertools

import jax
import jax.numpy as jnp
import numpy as np
from jax import lax
from jax.experimental import pallas as pl
from jax.experimental.pallas import tpu as pltpu

D_MODEL = 2048
PLE_DIM = 256
ATTN_WIDTH = 1024
CONV_WIDTH = 1024
HEAD_DIM = 64
N_Q_HEADS = 16
N_KV_HEADS = 4
KV_WIDTH = 256
WINDOW = 128
ROT_DIM = 16
ROPE_THETA = 500000.0
CONV_K = 3
EPS = 1e-6
NEG_INF = -1e30
IN_WIDTH = 2 * ATTN_WIDTH + 2 * KV_WIDTH + 4 * CONV_WIDTH

LANES = 128
SUBLANES = 8
VMEM_BYTES = 64 * 1024 * 1024
QBLK = 128

COL_Q, COL_K, COL_V, COL_GA, COL_B, COL_C, COL_H, COL_GC = 0, 1024, 1280, 1536, 2560, 3584, 4608, 5632
ATTN_COLS = COL_B
ZC_B, ZC_C, ZC_H, ZC_GC = (c - COL_B for c in (COL_B, COL_C, COL_H, COL_GC))
PROJ_CHUNK = 512
FRONT_TM = 256
FILL_PIECES = 8
TAIL_CHUNKS = 3
OUT_TM = 512
FRONT_VMEM_LIMIT = VMEM_BYTES - 6 * 1024 * 1024
OUT_VMEM_LIMIT = VMEM_BYTES - 2 * 1024 * 1024

F32 = jnp.float32
BF16 = jnp.bfloat16
LOG2E = 1.4426950408889634


def _silu_of_half(half):
    return half + half * jnp.tanh(half)


class _Attention:
    def __init__(self, sinks_ref, z_ref, tabs, bias_ref, qg, kg, mix_ref, kband, vband, first_of_seq, tm):
        self.sinks_ref, self.z_ref, self.tabs, self.qg, self.kg = sinks_ref, z_ref, tabs, qg, kg
        self.mix_ref, self.kband, self.vband, self.tm = mix_ref, kband, vband, tm
        self.units = [(qb, g) for qb in range(tm // QBLK) for g in range(N_KV_HEADS)]
        self.lo = lax.broadcasted_iota(jnp.int32, (1, LANES), 1) < HEAD_DIM
        rr = (lax.broadcasted_iota(jnp.int32, (2 * LANES, LANES), 0) % LANES) // HEAD_DIM
        cc = lax.broadcasted_iota(jnp.int32, (2 * LANES, LANES), 1) // HEAD_DIM
        self.bd = jnp.where(rr == cc, 1.0, 0.0).astype(BF16)
        self.bias = [bias_ref.at[first_of_seq.astype(jnp.int32)]] + [bias_ref.at[0]] * (tm // QBLK - 1)
        self.state = {}

    def _zcol(self, col, qb):
        return self.z_ref[qb * QBLK:(qb + 1) * QBLK, col:col + LANES]

    def _head_sumsq(self, t):
        sq = t * t
        hi = sq.astype(BF16)
        rest = (sq - hi.astype(F32)).astype(BF16)
        return jnp.dot(jnp.concatenate([hi, rest], axis=1), self.bd, preferred_element_type=F32)

    def _head_norm_rope(self, t, ss, gain, qb):
        rc, rs1, rs2 = (self.tabs[qb * QBLK:(qb + 1) * QBLK, j * LANES:(j + 1) * LANES] for j in range(3))
        y = t * lax.rsqrt(ss * (1.0 / HEAD_DIM) + EPS) * gain
        return y * rc + pltpu.roll(y, LANES - ROT_DIM // 2, 1) * rs1 + pltpu.roll(y, ROT_DIM // 2, 1) * rs2

    def _both_halves(self, t):
        tr = pltpu.roll(t, HEAD_DIM, 1)
        return jnp.where(self.lo, t, tr), jnp.where(self.lo, tr, t)

    def sumsq(self, u):
        qb, g = self.units[u]
        st = self.state[u] = {}
        if g == 0:
            st["k_raw"] = [self._zcol(COL_K + j * LANES, qb) for j in range(KV_WIDTH // LANES)]
            st["k_ss"] = [self._head_sumsq(t) for t in st["k_raw"]]
        st["q_raw"] = [self._zcol(COL_Q + (2 * g + c) * LANES, qb) for c in range(2)]
        st["q_ss"] = [self._head_sumsq(t) for t in st["q_raw"]]

    def scores(self, u):
        qb, g = self.units[u]
        st = self.state[u]
        rows = slice(QBLK + qb * QBLK, QBLK + (qb + 1) * QBLK)
        if g == 0:
            for j in range(KV_WIDTH // LANES):
                kt = self._head_norm_rope(st["k_raw"][j], st["k_ss"][j], self.kg, qb).T
                for half in range(2):
                    self.kband[2 * j + half, :, rows] = kt[half * HEAD_DIM:(half + 1) * HEAD_DIM].astype(BF16)
                for half, t in enumerate(self._both_halves(self._zcol(COL_V + j * LANES, qb))):
                    self.vband[2 * j + half, rows, :] = t.astype(BF16)
        q = jnp.concatenate(
            [self._head_norm_rope(t, ss, self.qg, qb).astype(BF16) for t, ss in zip(st["q_raw"], st["q_ss"])],
            axis=0)
        kt = self.kband[g, :, qb * QBLK:(qb + 2) * QBLK]
        zero = jnp.zeros_like(kt)
        st["s"] = [jnp.dot(q, jnp.concatenate([kt, zero], axis=0), preferred_element_type=F32),
                   jnp.dot(q, jnp.concatenate([zero, kt], axis=0), preferred_element_type=F32)]

    def output(self, u):
        qb, g = self.units[u]
        st = self.state.pop(u)
        vt = self.vband[g, qb * QBLK:(qb + 2) * QBLK, :]
        zero = jnp.zeros_like(vt)
        v2 = jnp.concatenate([jnp.where(self.lo, vt, zero), jnp.where(self.lo, zero, vt)], axis=0)
        for c in range(2):
            p = 2 * g + c
            es, inv = [], []
            for half in range(2):
                sink = self.sinks_ref[2 * p + half]
                sc = st["s"][half][c * QBLK:(c + 1) * QBLK] + self.bias[qb][...]
                m = jnp.maximum(jnp.max(sc, axis=-1, keepdims=True), sink)
                e = jnp.exp2(sc - m)
                den = jnp.sum(e, axis=-1, keepdims=True) + jnp.exp2(sink - m)
                es.append(e.astype(BF16))
                inv.append(1.0 / den)
            o = jnp.dot(jnp.concatenate(es, axis=1), v2, preferred_element_type=F32)
            o = o * jnp.where(self.lo, inv[0], inv[1])
            gate = self._zcol(COL_GA + p * LANES, qb)
            self.mix_ref[qb * QBLK:(qb + 1) * QBLK, p * LANES:(p + 1) * LANES] = (
                o * _silu_of_half(gate)).astype(BF16)

    def carry(self):
        self.kband[:, :, 0:QBLK] = self.kband[:, :, self.tm:self.tm + QBLK]
        self.vband[:, 0:QBLK, :] = self.vband[:, self.tm:self.tm + QBLK, :]


def _gated_conv(zc_ref, cw, u_scr, rows, n_rows):
    seg = lambda col: zc_ref[rows, col:col + CONV_WIDTH]
    u = seg(ZC_C) * seg(ZC_H)
    r0 = SUBLANES + rows.start
    u_scr[r0:r0 + n_rows, :] = u
    ext = u_scr[r0 - SUBLANES:r0 + n_rows, :]
    conv = (cw[0:1] * pltpu.roll(ext, 2, 0)[SUBLANES:, :] + cw[1:2] * pltpu.roll(ext, 1, 0)[SUBLANES:, :]
            + cw[2:3] * u)
    gc = seg(ZC_GC)
    return seg(ZC_B) * conv * _silu_of_half(gc)


def _front_kernel(sinks_ref, x_ref, w_ref, rope_ref, bias_ref, qg_ref, kg_ref,
                  mix_ref, zc_ref, h0, h1, z0, z1, kband, vband, *, tm, blocks_per_seq):
    i = pl.program_id(0)
    first_of_seq = (i + 2 * blocks_per_seq - 2) % blocks_per_seq == 0

    @pl.when(i == 0)
    def _():
        h1[...] = jnp.zeros_like(h1)
        z0[...] = jnp.zeros_like(z0)
        kband[...] = jnp.zeros_like(kband)
        vband[...] = jnp.zeros_like(vband)

    @pl.when(first_of_seq)
    def _():
        kband[:, :, 0:QBLK] = jnp.zeros((N_KV_HEADS, HEAD_DIM, QBLK), BF16)
        vband[:, 0:QBLK, :] = jnp.zeros((N_KV_HEADS, QBLK, LANES), BF16)

    def body(h_w, h_r, z_w, z_r):
        def norm_piece(r0, r1):
            x = x_ref[r0:r1, :]
            ms = jnp.mean(x * x, axis=-1, keepdims=True)
            h_w[r0:r1, :] = (x * lax.rsqrt(ms + EPS)).astype(BF16)

        rw_ = tm // FILL_PIECES
        fillers = iter([functools.partial(norm_piece, j * rw_, (j + 1) * rw_) for j in range(FILL_PIECES)])
        chunks = iter(range(IN_WIDTH // PROJ_CHUNK))

        def project(count):
            for n in itertools.islice(chunks, count):
                res = jnp.dot(h_r[...], w_ref[:, n * PROJ_CHUNK:(n + 1) * PROJ_CHUNK], preferred_element_type=F32)
                if n < ATTN_COLS // PROJ_CHUNK:
                    z_w[:, n * PROJ_CHUNK:(n + 1) * PROJ_CHUNK] = res
                else:
                    zc_ref[:, n * PROJ_CHUNK - ATTN_COLS:(n + 1) * PROJ_CHUNK - ATTN_COLS] = res

        def fill(count):
            for f in itertools.islice(fillers, count):
                f()

        att = _Attention(sinks_ref, z_r, rope_ref, bias_ref, qg_ref[...], kg_ref[...],
                         mix_ref, kband, vband, first_of_seq, tm)
        n_units = len(att.units)
        att.sumsq(0)
        project(1)
        for u in range(n_units):
            if u > 0:
                att.output(u - 1)
            att.scores(u)
            if u + 1 < n_units:
                att.sumsq(u + 1)
            project(1)
        att.output(n_units - 1)
        att.carry()
        project(1)
        for _ in range(TAIL_CHUNKS):
            project(1)
        fill(FILL_PIECES)

    @pl.when(i % 2 == 0)
    def _():
        body(h0, h1, z1, z0)

    @pl.when(i % 2 == 1)
    def _():
        body(h1, h0, z0, z1)


def _front(x2, w_in, sinks, tabs, bias, qg, kg, seq, tm):
    t = x2.shape[0]
    nblk = t // tm
    bps = seq // tm
    const = lambda r, c: pl.BlockSpec((r, c), lambda i: (0, 0), pipeline_mode=pl.Buffered(1))
    tab = pl.BlockSpec((tm, 3 * LANES), lambda i: ((i + 2 * bps - 2) % bps, 0))
    return pl.pallas_call(
        functools.partial(_front_kernel, tm=tm, blocks_per_seq=bps),
        grid=(nblk + 2,),
        in_specs=[
            pl.BlockSpec(memory_space=pltpu.SMEM),
            pl.BlockSpec((tm, D_MODEL), lambda i: (jnp.minimum(i, nblk - 1), 0)),
            const(D_MODEL, IN_WIDTH),
            tab,
            pl.BlockSpec((2, QBLK, 2 * QBLK), lambda i: (0, 0, 0), pipeline_mode=pl.Buffered(1)),
            const(1, LANES), const(1, LANES),
        ],
        out_specs=[
            pl.BlockSpec((tm, ATTN_WIDTH), lambda i: (jnp.maximum(i - 2, 0), 0)),
            pl.BlockSpec((tm, IN_WIDTH - ATTN_COLS), lambda i: (jnp.clip(i - 1, 0, nblk - 1), 0)),
        ],
        out_shape=[
            jax.ShapeDtypeStruct((t, ATTN_WIDTH), BF16),
            jax.ShapeDtypeStruct((t, IN_WIDTH - ATTN_COLS), F32),
        ],
        scratch_shapes=[
            pltpu.VMEM((tm, D_MODEL), BF16),
            pltpu.VMEM((tm, D_MODEL), BF16),
            pltpu.VMEM((tm, ATTN_COLS), F32),
            pltpu.VMEM((tm, ATTN_COLS), F32),
            pltpu.VMEM((N_KV_HEADS, HEAD_DIM, tm + QBLK), BF16),
            pltpu.VMEM((N_KV_HEADS, tm + QBLK, LANES), BF16),
        ],
        compiler_params=pltpu.CompilerParams(
            dimension_semantics=("arbitrary",),
            vmem_limit_bytes=FRONT_VMEM_LIMIT),
        name="front",
    )(sinks, x2, w_in, tabs, bias, qg, kg)


def _out_kernel(mixa_ref, zc_ref, x_ref, p_ref, cw_ref, wo_ref, wg_ref, bg_ref, wp_ref, g3_ref,
                o_ref, u_scr, *, blocks_per_seq):
    tm = o_ref.shape[0]

    @pl.when(pl.program_id(0) % blocks_per_seq == 0)
    def _():
        u_scr[0:SUBLANES, :] = jnp.zeros((SUBLANES, CONV_WIDTH), F32)

    halves = [slice(r * (tm // 2), (r + 1) * (tm // 2)) for r in range(2)]
    cw = cw_ref[...]
    pes = [jnp.dot(p_ref[rows, :].astype(BF16), wp_ref[...], preferred_element_type=F32) for rows in halves]
    for rows in halves:
        y_conv = _gated_conv(zc_ref, cw, u_scr, rows, tm // 2).astype(BF16)
        mix = jnp.concatenate([mixa_ref[rows, :], y_conv], axis=1)
        o_ref[rows, :] = x_ref[rows, :] + jnp.dot(mix, wo_ref[...], preferred_element_type=F32)
    u_scr[0:SUBLANES, :] = u_scr[tm:tm + SUBLANES, :]
    for rows, pe in zip(halves, pes):
        x1 = o_ref[rows, :]
        ms = jnp.mean(x1 * x1, axis=-1, keepdims=True)
        hn = (x1 * lax.rsqrt(ms + EPS)).astype(BF16)
        t = jnp.tanh(jnp.dot(hn, wg_ref[...], preferred_element_type=F32) + bg_ref[...])
        e_half = pe * lax.rsqrt(jnp.mean(pe * pe, axis=-1, keepdims=True) + EPS) * (0.5 * g3_ref[...])
        o_ref[rows, :] = (x1 + e_half) + e_half * t


def _out_stage(mixa, zc, x2, p2, conv_w, w_out, w_gate, b_gate, w_proj, g3, seq, tm):
    t = x2.shape[0]
    row = lambda w: pl.BlockSpec((tm, w), lambda i: (i, 0))
    const = lambda r, c: pl.BlockSpec((r, c), lambda i: (0, 0), pipeline_mode=pl.Buffered(1))
    return pl.pallas_call(
        functools.partial(_out_kernel, blocks_per_seq=seq // tm),
        grid=(t // tm,),
        in_specs=[
            row(ATTN_WIDTH), row(IN_WIDTH - ATTN_COLS), row(D_MODEL), row(PLE_DIM),
            const(CONV_K, CONV_WIDTH),
            const(D_MODEL, D_MODEL),
            const(D_MODEL, D_MODEL), const(1, D_MODEL),
            const(PLE_DIM, D_MODEL), const(1, D_MODEL),
        ],
        out_specs=row(D_MODEL),
        out_shape=jax.ShapeDtypeStruct((t, D_MODEL), F32),
        scratch_shapes=[pltpu.VMEM((tm + SUBLANES, CONV_WIDTH), F32)],
        compiler_params=pltpu.CompilerParams(
            dimension_semantics=("arbitrary",),
            vmem_limit_bytes=OUT_VMEM_LIMIT),
        name="outproj_ple",
    )(mixa, zc, x2, p2, conv_w, w_out, w_gate, b_gate, w_proj, g3)


def _rope_tables(s):
    half = ROT_DIM // 2
    inv_freq = jnp.power(jnp.float32(ROPE_THETA), -jnp.arange(half, dtype=F32) * 2.0 / ROT_DIM)
    ang = jnp.arange(s).astype(F32)[:, None] * inv_freq[None, :]
    cos, sin = jnp.cos(ang), jnp.sin(ang)
    pad = lambda n: jnp.zeros((s, n), F32)
    c64 = jnp.concatenate([cos, cos, jnp.ones((s, HEAD_DIM - ROT_DIM), F32)], axis=1)
    s1 = jnp.concatenate([-sin, pad(HEAD_DIM - half)], axis=1)
    s2 = jnp.concatenate([pad(half), sin, pad(HEAD_DIM - ROT_DIM)], axis=1)
    return jnp.concatenate([jnp.tile(t, (1, LANES // HEAD_DIM)) for t in (c64, s1, s2)], axis=1)


def _band_bias():
    qi = jnp.arange(QBLK)[:, None]
    ci = jnp.arange(2 * QBLK)[None, :]
    band = (ci > qi) & (ci <= qi + QBLK)
    masks = jnp.stack([band, band & (ci >= QBLK)])
    return jnp.where(masks, 0.0, NEG_INF).astype(F32)


def _layer(x, p_i, norm_gain, w_in, q_gain, k_gain, sinks, conv_w, w_out,
           gate_norm_gain, w_gate, b_gate, w_proj, ple_norm_gain):
    b, s, _ = x.shape
    t = b * s
    x2 = x.reshape(t, D_MODEL)
    qg = jnp.tile(q_gain * (HEAD_DIM ** -0.5 * LOG2E), LANES // HEAD_DIM).reshape(1, LANES)
    kg = jnp.tile(k_gain, LANES // HEAD_DIM).reshape(1, LANES)
    col_scale = np.ones((1, IN_WIDTH), np.float32)
    col_scale[:, COL_GA:COL_GA + ATTN_WIDTH] = 0.5
    col_scale[:, COL_GC:] = 0.5
    w_in_b = (w_in * norm_gain[:, None] * col_scale).astype(BF16)
    mixa, zc = _front(x2, w_in_b, sinks * LOG2E, _rope_tables(s), _band_bias(), qg, kg,
                      s, min(FRONT_TM, s))

    out = _out_stage(mixa, zc, x2, p_i.reshape(t, PLE_DIM), conv_w, w_out.astype(BF16),
                     (w_gate * (0.5 * gate_norm_gain)[:, None]).astype(BF16),
                     (0.5 * b_gate).reshape(1, D_MODEL), w_proj.astype(BF16),
                     ple_norm_gain.reshape(1, D_MODEL), s, min(OUT_TM, s))
    return out.reshape(b, s, D_MODEL)


def kernel(x, p, norm_gain, w_in, q_norm_gain, k_norm_gain, attn_sinks, conv_w, w_out,
           ple_gate_norm_gain, w_ple_gate, b_ple_gate, w_ple_proj, ple_norm_gain):
    for i in range(p.shape[0]):
        x = _layer(x, p[i], norm_gain[i], w_in[i], q_norm_gain[i], k_norm_gain[i], attn_sinks[i],
                   conv_w[i], w_out[i], ple_gate_norm_gain[i], w_ple_gate[i], b_ple_gate[i],
                   w_ple_proj[i], ple_norm_gain[i])
    return x
```
